```python
import jax, jax.numpy as jnp
from jax import lax
import numpy as np

D_MODEL = 1024
BATCH = 8
SEQ = 8192
DEPTH = 4
DEC_BATCH = 32
DEC_SEQ = 32
PAST_LEN = 1024

CHUNK = 64
N_META = 16
LEAD = CHUNK - N_META
QBLOCK = 128
ROPE_THETA = 500000.0
EPS = 1e-6
N_EVEN = (DEPTH + 1) // 2
N_ODD = DEPTH // 2
BRANCH_W = D_MODEL // 2
MIX_W = 2 * BRANCH_W

A_DH = 64
A_HEADS = BRANCH_W // A_DH
A_KV = 1
IDX_HEADS = 8
IDX_DIM = 32
TOPK_MAX = 256
B_DH = 64
B_HEADS = BRANCH_W // B_DH
C_DH = 64
C_HEADS = BRANCH_W // C_DH
C_KV = 1
WINDOW = 128
WIN_CHUNKS = WINDOW // CHUNK
D_DK = 128
D_DV = 128
D_HEADS = BRANCH_W // D_DK

EVEN_SIZES = (A_HEADS * A_DH, A_KV * A_DH, A_KV * A_DH, IDX_HEADS * IDX_DIM, IDX_DIM, IDX_HEADS, BRANCH_W,
              B_HEADS * B_DH, B_HEADS * B_DH, B_HEADS * B_DH, BRANCH_W)
ODD_SIZES = (C_HEADS * C_DH, C_KV * C_DH, C_KV * C_DH, BRANCH_W,
             D_HEADS * D_DK, D_HEADS * D_DK, D_HEADS * D_DV, BRANCH_W)
P_EVEN = sum(EVEN_SIZES)
P_ODD = sum(ODD_SIZES)

kernel_name = "hybrid_chunk_streaming_encoder_step"

F32 = jnp.float32


def split_cols(p, sizes):
    return jnp.split(p, [int(i) for i in np.cumsum(sizes)[:-1]], axis=-1)


def rmsnorm(x, g):
    xf = x.astype(F32)
    y = xf * lax.rsqrt(jnp.mean(xf * xf, axis=-1, keepdims=True) + EPS) * g.astype(F32)
    return y.astype(x.dtype)


def rope(x, pos):
    dh = x.shape[-1]
    r = dh // 4
    half = r // 2
    inv = ROPE_THETA ** (-(jnp.arange(half, dtype=F32) * 2.0 / r))
    ang = pos.astype(F32)[:, None] * inv[None, :]
    cos = jnp.cos(ang)[:, None, :]
    sin = jnp.sin(ang)[:, None, :]
    x1 = x[..., :half].astype(F32)
    x2 = x[..., half:r].astype(F32)
    return jnp.concatenate([(x1 * cos - x2 * sin).astype(x.dtype), (x2 * cos + x1 * sin).astype(x.dtype), x[..., r:]], axis=-1)


def chunk_id(pos):
    return jnp.where(pos < N_META, 0, 1 + (pos - N_META) // CHUNK)


def chunk_end(last):
    if last < N_META:
        return N_META
    return N_META + CHUNK * ((last - N_META) // CHUNK + 1)


def dsa_block(q, k, v, qi, ki, wi, q_pos, k_pos, topk):
    isc = jnp.einsum('bqhd,bkd->bqhk', qi.astype(F32), ki.astype(F32)) * (IDX_DIM ** -0.5)
    iscore = jnp.einsum('bqhk,bqh->bqk', jax.nn.relu(isc), wi.astype(F32))
    adm = chunk_id(k_pos)[None, :] <= chunk_id(q_pos)[:, None]
    iscore = jnp.where(adm[None], iscore, -jnp.inf)
    kk = min(topk, k.shape[1])
    vals, idx = lax.top_k(iscore, kk)
    valid = vals > -jnp.inf
    gather = jax.vmap(lambda rows, ids: rows[ids])
    ksel = gather(k, idx).astype(F32)
    vsel = gather(v, idx).astype(F32)
    b, tq, h, dh = q.shape
    qg = q.reshape(b, tq, A_KV, h // A_KV, dh).astype(F32)
    s = jnp.einsum('bqgrd,bqkgd->bqgrk', qg, ksel) * (dh ** -0.5)
    s = jnp.where(valid[:, :, None, None, :], s, -jnp.inf)
    p = jax.nn.softmax(s, axis=-1)
    o = jnp.einsum('bqgrk,bqkgd->bqgrd', p, vsel)
    return o.reshape(b, tq, h, dh).astype(q.dtype)


def dsa_prompt(q, k, v, qi, ki, wi, pos, topk):
    n = q.shape[1]
    outs = []
    for qs in range(0, n, QBLOCK):
        qe = min(qs + QBLOCK, n)
        ke = min(chunk_end(qe - 1), n)
        outs.append(dsa_block(q[:, qs:qe], k[:, :ke], v[:, :ke], qi[:, qs:qe], ki[:, :ke], wi[:, qs:qe],
                              pos[qs:qe], pos[:ke], topk))
    return jnp.concatenate(outs, axis=1)


def sb_block(q, k, v, q_pos, k_pos):
    dh = q.shape[-1]
    z = jnp.einsum('bqhd,bkhd->bhqk', q.astype(F32), k.astype(F32)) * (dh ** -0.5)
    strict = k_pos[None, :] < q_pos[:, None]
    log_not = jnp.where(strict, jax.nn.log_sigmoid(-z), 0.0)
    later = lax.cumsum(log_not, axis=3, reverse=True) - log_not
    a = jnp.where(strict, jnp.exp(jax.nn.log_sigmoid(z) + later), 0.0)
    o = jnp.einsum('bhqk,bkhd->bqhd', a, v.astype(F32))
    return o.astype(q.dtype)


def sb_prompt(q, k, v, pos):
    n = q.shape[1]
    outs = []
    for qs in range(0, n, QBLOCK):
        qe = min(qs + QBLOCK, n)
        outs.append(sb_block(q[:, qs:qe], k[:, :qe], v[:, :qe], pos[qs:qe], pos[:qe]))
    return jnp.concatenate(outs, axis=1)


def sink_attend(qg, k, v, sink, mask):
    dh = qg.shape[-1]
    s = jnp.einsum('...qgrd,...kgd->...grqk', qg.astype(F32), k.astype(F32)) * (dh ** -0.5)
    s = jnp.where(mask, s, -jnp.inf)
    sk = sink.astype(F32).reshape(qg.shape[-3], qg.shape[-2], 1, 1)
    m = jnp.maximum(jnp.max(s, axis=-1, keepdims=True), sk)
    p = jnp.exp(s - m)
    p = p / (jnp.sum(p, axis=-1, keepdims=True) + jnp.exp(sk - m))
    o = jnp.einsum('...grqk,...kgd->...qgrd', p, v.astype(F32))
    return o.astype(qg.dtype)


def swa_prompt(q, k, v, sink):
    b, n, h, dh = q.shape
    nc = (n + LEAD) // CHUNK
    rep = h // C_KV
    qp = jnp.pad(q, ((0, 0), (LEAD, 0), (0, 0), (0, 0))).reshape(b, nc, CHUNK, C_KV, rep, dh)
    kpad = LEAD + WIN_CHUNKS * CHUNK
    kp = jnp.pad(k, ((0, 0), (kpad, 0), (0, 0), (0, 0))).reshape(b, nc + WIN_CHUNKS, CHUNK, C_KV, dh)
    vp = jnp.pad(v, ((0, 0), (kpad, 0), (0, 0), (0, 0))).reshape(b, nc + WIN_CHUNKS, CHUNK, C_KV, dh)
    kwin = jnp.concatenate([kp[:, w:w + nc] for w in range(WIN_CHUNKS + 1)], axis=2)
    vwin = jnp.concatenate([vp[:, w:w + nc] for w in range(WIN_CHUNKS + 1)], axis=2)
    kvalid = (jnp.arange((nc + WIN_CHUNKS) * CHUNK) >= kpad).reshape(nc + WIN_CHUNKS, CHUNK)
    mwin = jnp.concatenate([kvalid[w:w + nc] for w in range(WIN_CHUNKS + 1)], axis=1)
    o = sink_attend(qp, kwin, vwin, sink, mwin[None, :, None, None, None, :])
    return o.reshape(b, nc * CHUNK, h, dh)[:, LEAD:]


def swa_sample(q, k, v, sink, q_pos, k_pos):
    b, t, h, dh = q.shape
    cq = chunk_id(q_pos)[:, None]
    ck = chunk_id(k_pos)[None, :]
    band = (ck <= cq) & (ck >= cq - WIN_CHUNKS)
    o = sink_attend(q.reshape(b, t, C_KV, h // C_KV, dh), k, v, sink, band[None, None, None])
    return o.reshape(b, t, h, dh)


def hgrn2_chunk(state, xs):
    q, logf, kk, i = xs
    c = q.shape[2]
    cum = jnp.cumsum(logf, axis=2)
    tri = jnp.tril(jnp.ones((c, c), dtype=bool))
    o_inter = jnp.einsum('bhtk,bhkv->bhtv', q * jnp.exp(cum), state)
    rel = jnp.where(tri[:, :, None], cum[:, :, :, None, :] - cum[:, :, None, :, :], -jnp.inf)
    att = jnp.einsum('bhtk,bhtsk,bhsk->bhts', q, jnp.exp(rel), kk)
    o = o_inter + jnp.einsum('bhts,bhsv->bhtv', att, i)
    last = cum[:, :, -1:, :]
    new_state = jnp.exp(last[:, :, 0, :])[..., None] * state + jnp.einsum('bhsk,bhsv->bhkv', kk * jnp.exp(last - cum), i)
    return new_state, o


def hgrn2_prompt(q, logf, kk, i):
    b, n, h, dk = q.shape
    dv = i.shape[-1]
    nc = (n + LEAD) // CHUNK

    def to_chunks(x):
        x = jnp.pad(x.astype(F32), ((0, 0), (LEAD, 0), (0, 0), (0, 0)))
        return x.reshape(b, nc, CHUNK, h, x.shape[-1]).transpose(1, 0, 3, 2, 4)

    s0 = jnp.zeros((b, h, dk, dv), F32)
    s_fin, o = lax.scan(hgrn2_chunk, s0, (to_chunks(q), to_chunks(logf), to_chunks(kk), to_chunks(i)))
    o = o.transpose(1, 0, 3, 2, 4).reshape(b, nc * CHUNK, h, dv)[:, LEAD:]
    return o, s_fin


def hgrn2_sample(q, logf, kk, i, state):
    tr = lambda x: x.astype(F32).transpose(0, 2, 1, 3)
    s_new, o = hgrn2_chunk(state.astype(F32), (tr(q), tr(logf), tr(kk), tr(i)))
    return o.transpose(0, 2, 1, 3), s_new


def even_mixers(hn, w_in, w_out, q_pos, k_pos, caches, topk):
    b, t, _ = hn.shape
    aq, ak, av, iq, ik, iw, ag, bq, bk, bv, bg = split_cols(hn @ w_in, EVEN_SIZES)
    aq = rope(aq.reshape(b, t, A_HEADS, A_DH), q_pos)
    ak = rope(ak.reshape(b, t, A_KV, A_DH), q_pos)
    av = av.reshape(b, t, A_KV, A_DH)
    iq = rope(iq.reshape(b, t, IDX_HEADS, IDX_DIM), q_pos)
    ik = rope(ik.reshape(b, t, 1, IDX_DIM), q_pos)[:, :, 0]
    iw = iw * (IDX_HEADS ** -0.5)
    bq = bq.reshape(b, t, B_HEADS, B_DH)
    bk = bk.reshape(b, t, B_HEADS, B_DH)
    bv = bv.reshape(b, t, B_HEADS, B_DH)
    if caches is None:
        ao = dsa_prompt(aq, ak, av, iq, ik, iw, q_pos, topk)
        bo = sb_prompt(bq, bk, bv, q_pos)
    else:
        cak, cav, caik, cbk, cbv = caches
        ao = dsa_block(aq, jnp.concatenate([cak, ak], 1), jnp.concatenate([cav, av], 1), iq,
                       jnp.concatenate([caik, ik], 1), iw, q_pos, k_pos, topk)
        bo = sb_block(bq, jnp.concatenate([cbk, bk], 1), jnp.concatenate([cbv, bv], 1), q_pos, k_pos)
    mixed = jnp.concatenate([ao.reshape(b, t, -1) * jax.nn.silu(ag), bo.reshape(b, t, -1) * jax.nn.silu(bg)], axis=-1)
    return mixed @ w_out, (ak, av, ik, bk, bv)


def odd_mixers(hn, w_in, w_out, sink, lb, d_g, q_pos, k_pos, caches):
    b, t, _ = hn.shape
    cq, ck, cv, cg, dq, df, di, dg = split_cols(hn @ w_in, ODD_SIZES)
    cq = rope(cq.reshape(b, t, C_HEADS, C_DH), q_pos)
    ck = rope(ck.reshape(b, t, C_KV, C_DH), q_pos)
    cv = cv.reshape(b, t, C_KV, C_DH)
    dq = dq.reshape(b, t, D_HEADS, D_DK).astype(F32)
    df = df.reshape(b, t, D_HEADS, D_DK).astype(F32)
    di = di.reshape(b, t, D_HEADS, D_DV).astype(F32)
    logf = jnp.logaddexp(jnp.log(lb), jnp.log1p(-lb) + jax.nn.log_sigmoid(df))
    kk = (1.0 - lb) * jax.nn.sigmoid(-df)
    if caches is None:
        co = swa_prompt(cq, ck, cv, sink)
        do, s_new = hgrn2_prompt(dq, logf, kk, di)
        new_ck, new_cv = ck[:, -WINDOW:], cv[:, -WINDOW:]
    else:
        cck, ccv, sd = caches
        kf = jnp.concatenate([cck, ck], 1)
        vf = jnp.concatenate([ccv, cv], 1)
        co = swa_sample(cq, kf, vf, sink, q_pos, k_pos)
        do, s_new = hgrn2_sample(dq, logf, kk, di, sd)
        new_ck, new_cv = kf[:, -WINDOW:], vf[:, -WINDOW:]
    do = rmsnorm(do, d_g).astype(hn.dtype)
    mixed = jnp.concatenate([co.reshape(b, t, -1) * jax.nn.silu(cg), do.reshape(b, t, -1) * jax.nn.silu(dg)], axis=-1)
    return mixed @ w_out, (new_ck, new_cv, s_new.astype(hn.dtype))


def setup_inputs(seed: int = 0) -> dict:
    key = jax.random.key(seed)
    ks = jax.random.split(key, 20)
    nrm = lambda k, shape, scale=1.0: jax.random.normal(k, shape, F32) * scale
    full_rows = N_META + PAST_LEN
    return {
        "x_prompt": nrm(ks[0], (BATCH, SEQ, D_MODEL)),
        "x_sample": nrm(ks[1], (DEC_BATCH, DEC_SEQ, D_MODEL)),
        "cache_a_k": nrm(ks[2], (N_EVEN, DEC_BATCH, full_rows, A_KV, A_DH)),
        "cache_a_v": nrm(ks[3], (N_EVEN, DEC_BATCH, full_rows, A_KV, A_DH)),
        "cache_a_ik": nrm(ks[4], (N_EVEN, DEC_BATCH, full_rows, IDX_DIM)),
        "cache_b_k": nrm(ks[5], (N_EVEN, DEC_BATCH, full_rows, B_HEADS, B_DH)),
        "cache_b_v": nrm(ks[6], (N_EVEN, DEC_BATCH, full_rows, B_HEADS, B_DH)),
        "cache_c_k": nrm(ks[7], (N_ODD, DEC_BATCH, WINDOW, C_KV, C_DH)),
        "cache_c_v": nrm(ks[8], (N_ODD, DEC_BATCH, WINDOW, C_KV, C_DH)),
        "state_d": nrm(ks[9], (N_ODD, DEC_BATCH, D_HEADS, D_DK, D_DV), 0.5),
        "meta_tokens": nrm(ks[10], (N_META, D_MODEL)),
        "norm_g": 1.0 + nrm(ks[11], (DEPTH, D_MODEL), 0.05),
        "final_g": 1.0 + nrm(ks[12], (D_MODEL,), 0.05),
        "w_in_even": nrm(ks[13], (N_EVEN, D_MODEL, P_EVEN), D_MODEL ** -0.5),
        "w_out_even": nrm(ks[14], (N_EVEN, MIX_W, D_MODEL), MIX_W ** -0.5),
        "w_in_odd": nrm(ks[15], (N_ODD, D_MODEL, P_ODD), D_MODEL ** -0.5),
        "w_out_odd": nrm(ks[16], (N_ODD, MIX_W, D_MODEL), MIX_W ** -0.5),
        "c_sinks": nrm(ks[17], (N_ODD, C_HEADS), 0.5),
        "d_lb_raw": nrm(ks[18], (DEPTH, D_HEADS * D_DK), 0.5),
        "d_norm_g": 1.0 + nrm(ks[19], (N_ODD, D_DV), 0.05),
    }


def reference(x_prompt, x_sample, cache_a_k, cache_a_v, cache_a_ik, cache_b_k, cache_b_v,
              cache_c_k, cache_c_v, state_d, meta_tokens, norm_g, final_g,
              w_in_even, w_out_even, w_in_odd, w_out_odd, c_sinks, d_lb_raw, d_norm_g):
    bp, seq, _ = x_prompt.shape
    tdec = x_sample.shape[1]
    past = cache_a_k.shape[2] - N_META
    n = N_META + seq
    pos_p = jnp.arange(n)
    start = N_META + past
    pos_s = start + jnp.arange(tdec)
    kpos_full = jnp.arange(start + tdec)
    kpos_win = jnp.arange(start - WINDOW, start + tdec)
    topk_p = min(TOPK_MAX, seq // 4)
    topk_s = min(TOPK_MAX, (past + tdec) // 4)
    lbp = jax.nn.softmax(d_lb_raw.astype(F32), axis=0)
    lower = jnp.cumsum(lbp, axis=0) - lbp[0]

    hp = jnp.concatenate([jnp.broadcast_to(meta_tokens[None].astype(x_prompt.dtype), (bp, N_META, D_MODEL)), x_prompt], axis=1)
    hs = x_sample
    even_p, even_s, odd_p, odd_s = [], [], [], []
    for l in range(DEPTH):
        j = l // 2
        hnp = rmsnorm(hp, norm_g[l])
        hns = rmsnorm(hs, norm_g[l])
        if l % 2 == 0:
            dp, newp = even_mixers(hnp, w_in_even[j], w_out_even[j], pos_p, pos_p, None, topk_p)
            ds, news = even_mixers(hns, w_in_even[j], w_out_even[j], pos_s, kpos_full,
                                   (cache_a_k[j], cache_a_v[j], cache_a_ik[j], cache_b_k[j], cache_b_v[j]), topk_s)
            even_p.append(newp)
            even_s.append(news)
        else:
            lb = lower[l].reshape(D_HEADS, D_DK)
            dp, newp = odd_mixers(hnp, w_in_odd[j], w_out_odd[j], c_sinks[j], lb, d_norm_g[j], pos_p, pos_p, None)
            ds, news = odd_mixers(hns, w_in_odd[j], w_out_odd[j], c_sinks[j], lb, d_norm_g[j], pos_s, kpos_win,
                                  (cache_c_k[j], cache_c_v[j], state_d[j]))
            odd_p.append(newp)
            odd_s.append(news)
        hp = hp + dp
        hs = hs + ds

    y_prompt = rmsnorm(hp, final_g)[:, N_META:]
    y_sample = rmsnorm(hs, final_g)
    stk = lambda group, idx: jnp.stack([g[idx] for g in group])
    return (y_prompt, y_sample,
            stk(even_p, 0), stk(even_s, 0), stk(even_p, 1), stk(even_s, 1), stk(even_p, 2), stk(even_s, 2),
            stk(even_p, 3), stk(even_s, 3), stk(even_p, 4), stk(even_s, 4),
            stk(odd_p, 0), stk(odd_s, 0), stk(odd_p, 1), stk(odd_s, 1), stk(odd_p, 2), stk(odd_s, 2))
```

```python
import functools

import numpy as np
import jax
import jax.numpy as jnp
from jax import lax
from jax.experimental import pallas as pl
from jax.experimental.pallas import tpu as pltpu

F32 = jnp.float32
BF16 = jnp.bfloat16
I32 = jnp.int32

CHUNK = 64
N_META = 16
FRONT = CHUNK - N_META
ROPE_THETA = 500000.0
EPS = 1e-6
HEAD_DH = 64
N_HEADS = 8
IDX_HEADS = 8
IDX_DIM = 32
TOPK_MAX = 256
WIN_CHUNKS = 2
D_HEADS = 4
D_DK = 128

LANES = 128
TQ = 128
TK = 256
SEQ_ALIGN = 256
VMEM_LIMIT = 56 * 1024 * 1024

NEG = -1e30
INT_MIN = -2 ** 31


def _nt(a, b):
    return lax.dot_general(a, b, (((1,), (1,)), ((), ())), preferred_element_type=F32)


def _mm(a, b):
    return jnp.dot(a, b, preferred_element_type=F32)


def _split3(x):
    hi = x.astype(BF16)
    r1 = x - hi.astype(F32)
    mid = r1.astype(BF16)
    lo = (r1 - mid.astype(F32)).astype(BF16)
    return hi, mid, lo


def _lane_iota(shape):
    return lax.broadcasted_iota(I32, shape, len(shape) - 1)


def _row_iota(shape):
    return lax.broadcasted_iota(I32, shape, len(shape) - 2)


def _rope128(y, c, s1, s2, shift):
    return y * c + pltpu.roll(y, LANES - shift, 1) * s1 + pltpu.roll(y, shift, 1) * s2


def _proj_kernel(x_ref, g_ref, w_ref, tab_ref, *out_refs, segs, iw_scale):
    x = x_ref[...]
    ms = jnp.mean(x * x, axis=-1, keepdims=True)
    xn = (x * lax.rsqrt(ms + EPS) * g_ref[...]).astype(BF16)
    tm = x.shape[0]
    lane = _lane_iota((tm, LANES))
    for (c0, width, kind), o_ref in zip(segs, out_refs):
        y = _mm(xn, w_ref[:, c0:c0 + width])
        if kind == "none":
            o_ref[...] = y
            continue
        if kind in ("ropeA_q", "ropeA_kv"):
            c, s1, s2 = (tab_ref[:, i * LANES:(i + 1) * LANES] for i in (0, 1, 2))
            shift = HEAD_DH // 8
        else:
            c, s1, s2 = (tab_ref[:, i * LANES:(i + 1) * LANES] for i in (3, 4, 5))
            shift = IDX_DIM // 8
        if kind == "ropeA_kv":
            keep = lane < HEAD_DH
            c, s1, s2 = jnp.where(keep, c, 1.0), jnp.where(keep, s1, 0.0), jnp.where(keep, s2, 0.0)
        if kind == "ropeI_kw":
            keep = lane < IDX_DIM
            c = jnp.where(keep, c, jnp.where(lane < IDX_DIM + IDX_HEADS, iw_scale, 1.0))
            s1, s2 = jnp.where(keep, s1, 0.0), jnp.where(keep, s2, 0.0)
        for g in range(width // LANES):
            r = _rope128(y[:, g * LANES:(g + 1) * LANES], c, s1, s2, shift)
            if kind == "ropeA_q":
                r = r * (HEAD_DH ** -0.5)
            o_ref[:, g * LANES:(g + 1) * LANES] = r


def _proj(x, g, w, tab, segs, tm):
    b, t, d = x.shape
    pc = w.shape[1]
    kern = functools.partial(_proj_kernel, segs=segs, iw_scale=float(IDX_DIM ** -0.5 * IDX_HEADS ** -0.5))
    return pl.pallas_call(
        kern,
        grid=(b, t // tm),
        in_specs=[
            pl.BlockSpec((None, tm, d), lambda bi, i: (bi, i, 0)),
            pl.BlockSpec((1, d), lambda bi, i: (0, 0)),
            pl.BlockSpec((d, pc), lambda bi, i: (0, 0)),
            pl.BlockSpec((tm, tab.shape[1]), lambda bi, i: (i, 0)),
        ],
        out_specs=[pl.BlockSpec((None, tm, wd), lambda bi, i: (bi, i, 0)) for (_, wd, _) in segs],
        out_shape=[jax.ShapeDtypeStruct((b, t, wd), F32) for (_, wd, _) in segs],
        compiler_params=pltpu.CompilerParams(
            dimension_semantics=("parallel", "parallel"), vmem_limit_bytes=VMEM_LIMIT),
        name="in_proj",
    )(x, g, w, tab)


EVEN_SEGS = ((0, 512, "ropeA_q"), (512, 128, "ropeA_kv"), (640, 256, "ropeI_q"), (896, 128, "ropeI_kw"),
             (1024, 512, "none"), (1536, 512, "none"), (2048, 512, "none"), (2560, 512, "none"),
             (3072, 512, "none"))
ODD_SEGS = ((0, 512, "ropeA_q"), (512, 128, "ropeA_kv"), (640, 512, "none"), (1152, 512, "none"),
            (1664, 512, "none"), (2176, 512, "none"), (2688, 512, "none"))


def _rope_tables(pos):
    posf = pos.astype(F32)[:, None]
    lane = np.arange(LANES)
    out = []
    for dh in (HEAD_DH, IDX_DIM):
        r = dh // 4
        half = r // 2
        inv = ROPE_THETA ** (-(jnp.arange(half, dtype=F32) * 2.0 / r))
        ang = posf * inv[None, :]
        cos, sin = jnp.cos(ang), jnp.sin(ang)
        jj = lane % dh
        fi = jj % half
        cos_l, sin_l = cos[:, fi], sin[:, fi]
        out.append(jnp.where((jj < r)[None], cos_l, 1.0))
        out.append(jnp.where((jj < half)[None], -sin_l, 0.0))
        out.append(jnp.where(((jj >= half) & (jj < r))[None], sin_l, 0.0))
    return jnp.concatenate(out, axis=1)


def _silu(g):
    return g / (1.0 + jnp.exp(-g))


def _outproj_kernel(h_ref, o1_ref, g1_ref, o2_ref, g2_ref, w_ref, fg_ref, out_ref, *, lo, hi, final):
    tm = h_ref.shape[0]
    half = o1_ref.shape[1]
    m1 = (o1_ref[...] * _silu(g1_ref[...])).astype(BF16)
    m2 = (o2_ref[...] * _silu(g2_ref[...])).astype(BF16)
    hn = h_ref[...] + _mm(m1, w_ref[0:half, :]) + _mm(m2, w_ref[half:2 * half, :])
    if final:
        ms = jnp.mean(hn * hn, axis=-1, keepdims=True)
        hn = hn * lax.rsqrt(ms + EPS) * fg_ref[...]
    row = pl.program_id(1) * tm + _row_iota((tm, 1))
    out_ref[...] = jnp.where((row >= lo) & (row < hi), hn, 0.0)


def _outproj(h, o1, g1, o2, g2, w, fg, lo, hi, final, tm):
    b, t, d = h.shape
    half = o1.shape[2]
    row = lambda wd: pl.BlockSpec((None, tm, wd), lambda bi, i: (bi, i, 0))
    return pl.pallas_call(
        functools.partial(_outproj_kernel, lo=lo, hi=hi, final=final),
        grid=(b, t // tm),
        in_specs=[row(d), row(half), row(half), row(half), row(half),
                  pl.BlockSpec((2 * half, d), lambda bi, i: (0, 0)),
                  pl.BlockSpec((1, d), lambda bi, i: (0, 0))],
        out_specs=row(d),
        out_shape=jax.ShapeDtypeStruct((b, t, d), F32),
        compiler_params=pltpu.CompilerParams(
            dimension_semantics=("parallel", "parallel"), vmem_limit_bytes=VMEM_LIMIT),
        name="out_proj",
    )(h, o1, g1, o2, g2, w, fg)


def _stage_heads(q_ref, qh_ref):
    tq = q_ref.shape[0]
    lane = _lane_iota((tq, LANES))
    for hp in range(N_HEADS // 2):
        pair = q_ref[:, hp * LANES:(hp + 1) * LANES]
        qh_ref[2 * hp] = jnp.where(lane < HEAD_DH, pair, 0.0).astype(BF16)
        qh_ref[2 * hp + 1] = jnp.where(lane < HEAD_DH, pltpu.roll(pair, HEAD_DH, 1), 0.0).astype(BF16)


def _kv_tiles(kvt):
    lane = _lane_iota(kvt.shape)
    vaug = jnp.where(lane < HEAD_DH, pltpu.roll(kvt, HEAD_DH, 1), 1.0)
    return kvt.astype(BF16), vaug.astype(BF16)


def _merge_heads(res, out_ref):
    lane = _lane_iota(res[0].shape)
    for hp in range(N_HEADS // 2):
        out_ref[:, hp * LANES:(hp + 1) * LANES] = jnp.where(
            lane < HEAD_DH, res[2 * hp], pltpu.roll(res[2 * hp + 1], HEAD_DH, 1))


def _dsa_kernel(iq_ref, wq_ref, aq_ref, ik_ref, kv_ref, out_ref,
                sc_ref, wb_ref, qh_ref, m_ref, acc_ref, *, q0, k_lo, k_hi, topk):
    tq = iq_ref.shape[0]
    kp = ik_ref.shape[0]
    i = pl.program_id(1)
    qbase = q0 + i * tq
    nk = jnp.minimum(kp // TK, (qbase + tq + TK - 1) // TK)
    pq = qbase + _row_iota((tq, 1))
    cq = pq // CHUNK
    n_adm = jnp.clip((cq + 1) * CHUNK, k_lo, k_hi) - k_lo
    kq = jnp.minimum(topk, n_adm).astype(F32)

    for h in range(IDX_HEADS):
        wb_ref[h] = jnp.broadcast_to(wq_ref[:, IDX_DIM + h:IDX_DIM + h + 1], (tq, LANES))
    _stage_heads(aq_ref, qh_ref)
    iqb = [iq_ref[:, h * IDX_DIM:(h + 1) * IDX_DIM].astype(BF16) for h in range(IDX_HEADS)]

    def score_body(j, carry):
        k0 = pl.multiple_of(j * TK, TK)
        kib = ik_ref[pl.ds(k0, TK), 0:IDX_DIM].astype(BF16)
        for s in range(TK // LANES):
            kis = kib[s * LANES:(s + 1) * LANES]
            tot = jnp.zeros((tq, LANES), F32)
            for h in range(IDX_HEADS):
                tot = tot + jnp.maximum(_nt(iqb[h], kis), 0.0) * wb_ref[h]
            pk = k0 + s * LANES + _lane_iota((tq, LANES))
            adm = (pk // CHUNK <= cq) & (pk >= k_lo) & (pk < k_hi)
            bits = pltpu.bitcast(tot, I32)
            key = bits ^ ((bits >> 31) & 0x7FFFFFFF)
            sc_ref[:, pl.ds(pl.multiple_of(k0 + s * LANES, LANES), LANES)] = jnp.where(adm, key, INT_MIN)
        return carry
    lax.fori_loop(0, nk, score_body, 0)

    def count_ge(cand):
        cb = jnp.broadcast_to(cand, (tq, LANES))

        def body(j, cnt):
            for s in range(TK // LANES):
                x = sc_ref[:, pl.ds(pl.multiple_of(j * TK + s * LANES, LANES), LANES)]
                cnt = cnt + jnp.where(x >= cb, 1.0, 0.0)
            return cnt
        cnt = lax.fori_loop(0, nk, body, jnp.zeros((tq, LANES), F32))
        return jnp.sum(cnt, axis=1, keepdims=True)

    base = jnp.where(count_ge(jnp.zeros((tq, 1), I32)) >= kq, 0, INT_MIN).astype(I32)

    def bit_body(bi, base):
        cand = base | jnp.left_shift(jnp.int32(1), 30 - bi)
        return jnp.where(count_ge(cand) >= kq, cand, base)
    thr = lax.fori_loop(0, 31, bit_body, base)
    need = kq - count_ge(thr + 1)

    m_ref[...] = jnp.full(m_ref.shape, NEG, F32)
    acc_ref[...] = jnp.zeros(acc_ref.shape, F32)
    ustrict = (_row_iota((TK, TK)) < _lane_iota((TK, TK))).astype(BF16)

    def attn_body(j, run):
        k0 = pl.multiple_of(j * TK, TK)
        x = sc_ref[:, pl.ds(k0, TK)]
        eq = x == thr
        eqf = jnp.where(eq, 1.0, 0.0)
        before = _mm(eqf.astype(BF16), ustrict) + run
        sel = (x > thr) | (eq & (before < need))
        kb, vaug = _kv_tiles(kv_ref[pl.ds(k0, TK), :])
        for h in range(N_HEADS):
            s = jnp.where(sel, _nt(qh_ref[h], kb), NEG)
            m_prev = m_ref[h]
            m_col = jnp.maximum(m_prev[:, 0:1], jnp.max(s, axis=1, keepdims=True))
            m_new = jnp.broadcast_to(m_col, (tq, LANES))
            p = jnp.exp(s - m_col)
            acc_ref[h] = jnp.exp(m_prev - m_new) * acc_ref[h] + _mm(p.astype(BF16), vaug)
            m_ref[h] = m_new
        return run + jnp.sum(eqf, axis=1, keepdims=True)
    lax.fori_loop(0, nk, attn_body, jnp.zeros((tq, 1), F32))

    res = []
    for h in range(N_HEADS):
        a = acc_ref[h]
        res.append(a / jnp.broadcast_to(a[:, HEAD_DH:HEAD_DH + 1], (tq, LANES)))
    _merge_heads(res, out_ref)


def _dsa(iq, wq, aq, ik, kv, *, q0, k_lo, k_hi, topk):
    b, nq, _ = iq.shape
    kp = ik.shape[1]
    qspec = lambda wd: pl.BlockSpec((None, TQ, wd), lambda bi, i: (bi, i, 0))
    kspec = pl.BlockSpec((None, kp, LANES), lambda bi, i: (bi, 0, 0))
    return pl.pallas_call(
        functools.partial(_dsa_kernel, q0=q0, k_lo=k_lo, k_hi=k_hi, topk=topk),
        grid=(b, nq // TQ),
        in_specs=[qspec(iq.shape[2]), qspec(LANES), qspec(aq.shape[2]), kspec, kspec],
        out_specs=qspec(aq.shape[2]),
        out_shape=jax.ShapeDtypeStruct(aq.shape, F32),
        scratch_shapes=[pltpu.VMEM((TQ, kp), I32),
                        pltpu.VMEM((IDX_HEADS, TQ, LANES), F32),
                        pltpu.VMEM((N_HEADS, TQ, LANES), BF16),
                        pltpu.VMEM((N_HEADS, TQ, LANES), F32),
                        pltpu.VMEM((N_HEADS, TQ, LANES), F32)],
        compiler_params=pltpu.CompilerParams(
            dimension_semantics=("parallel", "parallel"), vmem_limit_bytes=VMEM_LIMIT),
        name="dsa_attn",
    )(iq, wq, aq, ik, kv)


def _sb_kernel(q_ref, k_ref, v_ref, out_ref, *, q0, k_lo):
    tq = q_ref.shape[0]
    i = pl.program_id(2)
    qbase = q0 + i * tq
    pq = qbase + _row_iota((tq, 1))
    jmax = (qbase + tq - 1) // TK
    lane = _lane_iota((tq, LANES))
    qpair = q_ref[...] * (HEAD_DH ** -0.5)
    qh = [jnp.where(lane < HEAD_DH, qpair, 0.0).astype(BF16),
          jnp.where(lane >= HEAD_DH, qpair, 0.0).astype(BF16)]
    ugt = (_row_iota((TK, TK)) > _lane_iota((TK, TK))).astype(BF16)

    def body(t, carry):
        j = jmax - t
        k0 = pl.multiple_of(j * TK, TK)
        kb = k_ref[pl.ds(k0, TK), :].astype(BF16)
        vb = v_ref[pl.ds(k0, TK), :].astype(BF16)
        pk = k0 + _lane_iota((tq, TK))
        strict = (pk < pq) & (pk >= k_lo)
        new = []
        for h in range(2):
            run, acc = carry[h]
            z = _nt(qh[h], kb)
            ls = jnp.minimum(-z, 0.0) - jnp.log1p(jnp.exp(-jnp.abs(z)))
            lm = jnp.where(strict, ls, 0.0)
            hi, mid, lo = _split3(lm)
            later = _mm(hi, ugt) + _mm(mid, ugt) + _mm(lo, ugt)
            a = jnp.exp(jnp.where(strict, z + ls + later + run, NEG))
            acc = acc + _mm(a.astype(BF16), vb)
            run = run + later[:, 0:1] + lm[:, 0:1]
            new.append((run, acc))
        return tuple(new)

    zero = (jnp.zeros((tq, 1), F32), jnp.zeros((tq, LANES), F32))
    (_, acc0), (_, acc1) = lax.fori_loop(0, jmax + 1, body, (zero, zero))
    out_ref[...] = jnp.where(lane < HEAD_DH, acc0, acc1)


def _sb(q, k, v, *, q0, k_lo):
    b, nq, w = q.shape
    kp = k.shape[1]
    return pl.pallas_call(
        functools.partial(_sb_kernel, q0=q0, k_lo=k_lo),
        grid=(b, w // LANES, nq // TQ),
        in_specs=[pl.BlockSpec((None, TQ, LANES), lambda bi, hp, i: (bi, i, hp)),
                  pl.BlockSpec((None, kp, LANES), lambda bi, hp, i: (bi, 0, hp)),
                  pl.BlockSpec((None, kp, LANES), lambda bi, hp, i: (bi, 0, hp))],
        out_specs=pl.BlockSpec((None, TQ, LANES), lambda bi, hp, i: (bi, i, hp)),
        out_shape=jax.ShapeDtypeStruct(q.shape, F32),
        compiler_params=pltpu.CompilerParams(
            dimension_semantics=("parallel", "parallel", "parallel"), vmem_limit_bytes=VMEM_LIMIT),
        name="sb_attn",
    )(q, k, v)


def _swa_kernel(sink_ref, q_ref, kv0_ref, kv1_ref, out_ref, qh_ref, *, q0, k_lo, k_hi):
    tq = q_ref.shape[0]
    i = pl.program_id(1)
    qbase = q0 + i * tq
    cq = (qbase + _row_iota((tq, 1))) // CHUNK
    pk = qbase + _lane_iota((tq, 2 * tq))
    ck = pk // CHUNK - tq // CHUNK
    mask = (ck <= cq) & (ck >= cq - WIN_CHUNKS) & (pk >= k_lo + tq) & (pk < k_hi + tq)
    _stage_heads(q_ref, qh_ref)
    kb0, va0 = _kv_tiles(kv0_ref[...])
    kb1, va1 = _kv_tiles(kv1_ref[...])
    kb = jnp.concatenate([kb0, kb1], axis=0)
    va = jnp.concatenate([va0, va1], axis=0)
    res = []
    for h in range(N_HEADS):
        s = jnp.where(mask, _nt(qh_ref[h], kb), NEG)
        sk = sink_ref[h]
        m = jnp.maximum(jnp.max(s, axis=1, keepdims=True), sk)
        p = jnp.exp(s - m)
        pv = _mm(p.astype(BF16), va)
        den = pv[:, HEAD_DH:HEAD_DH + 1] + jnp.exp(sk - m)
        res.append(pv / jnp.broadcast_to(den, (tq, LANES)))
    _merge_heads(res, out_ref)


def _swa(sink, q, kv, *, q0, k_lo, k_hi):
    b, nq, w = q.shape
    return pl.pallas_call(
        functools.partial(_swa_kernel, q0=q0, k_lo=k_lo, k_hi=k_hi),
        grid=(b, nq // TQ),
        in_specs=[pl.BlockSpec(memory_space=pltpu.SMEM),
                  pl.BlockSpec((None, TQ, w), lambda bi, i: (bi, i, 0)),
                  pl.BlockSpec((None, TQ, LANES), lambda bi, i: (bi, i, 0)),
                  pl.BlockSpec((None, TQ, LANES), lambda bi, i: (bi, i + 1, 0))],
        out_specs=pl.BlockSpec((None, TQ, w), lambda bi, i: (bi, i, 0)),
        out_shape=jax.ShapeDtypeStruct(q.shape, F32),
        scratch_shapes=[pltpu.VMEM((N_HEADS, TQ, LANES), BF16)],
        compiler_params=pltpu.CompilerParams(
            dimension_semantics=("parallel", "parallel"), vmem_limit_bytes=VMEM_LIMIT),
        name="swa_attn",
    )(sink, q, kv, kv)


HG_LEVELS = (32, 16, 8, 4, 2, 1)
HG_BB = 8


def _hgrn_consts():
    t = np.arange(CHUNK)
    tri = (t[None, :] <= t[:, None]).astype(np.float32)
    rows = [tri]
    for m in HG_LEVELS:
        ref = (t // (2 * m)) * (2 * m) + m - 1
        rows.append(tri[ref])
    return np.concatenate(rows, axis=0)


def _hgrn_kernel(q_ref, f_ref, i_ref, lb_ref, gn_ref, gm_ref, s0_ref, o_ref, sT_ref, st_ref, *, p0, lo, hi):
    c = pl.program_id(1)
    nb = q_ref.shape[0]

    @pl.when(c == 0)
    def _():
        st_ref[...] = s0_ref[...]

    row = p0 + c * CHUNK + _row_iota((CHUNK, 1))
    valid = (row >= lo) & (row < hi)
    tt = _row_iota((CHUNK, CHUNK))
    ss = _lane_iota((CHUNK, CHUNK))
    lmask = [(tt // (2 * m) == ss // (2 * m)) & (tt % (2 * m) >= m) & (ss % (2 * m) < m) for m in HG_LEVELS]
    gmat = gm_ref[...]

    def per_batch(b, carry):
        for h in range(D_HEADS):
            cs = slice(h * D_DK, (h + 1) * D_DK)
            lb = lb_ref[:, cs]
            x = f_ref[b, :, cs]
            e = jnp.exp(-jnp.abs(x))
            r = 1.0 / (1.0 + e)
            sig = jnp.where(x >= 0, r, e * r)
            nsig = jnp.where(x >= 0, e * r, r)
            logf = jnp.where(valid, jnp.log(lb + (1.0 - lb) * sig), 0.0)
            kk = jnp.where(valid, (1.0 - lb) * nsig, 0.0)
            q = jnp.where(valid, q_ref[b, :, cs], 0.0)
            iv = jnp.where(valid, i_ref[b, :, cs], 0.0)
            hi3, mid3, lo3 = _split3(logf)
            cums = _mm(gmat, hi3) + _mm(gmat, mid3) + _mm(gmat, lo3)
            cum = cums[0:CHUNK]
            last = cum[CHUNK - 1:CHUNK]
            att = jnp.zeros((CHUNK, CHUNK), F32)
            for li in range(len(HG_LEVELS)):
                ref = cums[(li + 1) * CHUNK:(li + 2) * CHUNK]
                qm = q * jnp.exp(jnp.minimum(cum - ref, 0.0))
                km = kk * jnp.exp(jnp.minimum(ref - cum, 0.0))
                att = att + jnp.where(lmask[li], _nt(qm.astype(BF16), km.astype(BF16)), 0.0)
            ivb = iv.astype(BF16)
            stT = st_ref[b, h]
            o = _nt((q * jnp.exp(cum)).astype(BF16), stT.astype(BF16))
            o = o + _mm(att.astype(BF16), ivb) + jnp.sum(q * kk, axis=1, keepdims=True) * iv
            kd = (kk * jnp.exp(last - cum)).astype(BF16)
            st_ref[b, h] = stT * jnp.exp(last) + _mm(iv.T.astype(BF16), kd)
            ms = jnp.mean(o * o, axis=1, keepdims=True)
            o_ref[b, :, cs] = o * lax.rsqrt(ms + EPS) * gn_ref[:, cs]
        return carry
    lax.fori_loop(0, nb, per_batch, 0)

    @pl.when(c == pl.num_programs(1) - 1)
    def _():
        sT_ref[...] = st_ref[...]


def _hgrn(dq, df, di, lb, gn, s0T, *, p0, lo, hi):
    b, t, w = dq.shape
    bb = min(HG_BB, b)
    gm = jnp.asarray(_hgrn_consts(), BF16)
    row = pl.BlockSpec((bb, CHUNK, w), lambda bi, c: (bi, c, 0))
    vec = pl.BlockSpec((1, w), lambda bi, c: (0, 0))
    sspec = pl.BlockSpec((bb,) + s0T.shape[1:], lambda bi, c: (bi, 0, 0, 0))
    return pl.pallas_call(
        functools.partial(_hgrn_kernel, p0=p0, lo=lo, hi=hi),
        grid=(b // bb, t // CHUNK),
        in_specs=[row, row, row, vec, vec, pl.BlockSpec(gm.shape, lambda bi, c: (0, 0)), sspec],
        out_specs=[row, sspec],
        out_shape=[jax.ShapeDtypeStruct(dq.shape, F32), jax.ShapeDtypeStruct(s0T.shape, F32)],
        scratch_shapes=[pltpu.VMEM((bb,) + s0T.shape[1:], F32)],
        compiler_params=pltpu.CompilerParams(
            dimension_semantics=("parallel", "arbitrary"), vmem_limit_bytes=VMEM_LIMIT),
        name="hgrn2",
    )(dq, df, di, lb, gn, gm, s0T)


def _pick_tile(t, cap):
    best = 8
    for d in range(8, cap + 1, 8):
        if t % d == 0:
            best = d
    return best


def _pad_rows(x, front, total):
    return jnp.pad(x, ((0, 0), (front, total - front - x.shape[1]), (0, 0)))


def kernel(x_prompt, x_sample, cache_a_k, cache_a_v, cache_a_ik, cache_b_k, cache_b_v, cache_c_k, cache_c_v, state_d, meta_tokens, norm_g, final_g, w_in_even, w_out_even, w_in_odd, w_out_odd, c_sinks, d_lb_raw, d_norm_g):
    bp, seq, d = x_prompt.shape
    bs, tdec, _ = x_sample.shape
    depth = norm_g.shape[0]
    window = cache_c_k.shape[2]
    past = cache_a_k.shape[2] - N_META
    n = N_META + seq
    start = N_META + past
    topk_p = min(TOPK_MAX, seq // 4)
    topk_s = min(TOPK_MAX, (past + tdec) // 4)
    assert window == WIN_CHUNKS * CHUNK and TQ == WIN_CHUNKS * CHUNK

    np_ = -(-(FRONT + n) // SEQ_ALIGN) * SEQ_ALIGN
    p_lo, p_hi = FRONT, FRONT + n
    hp = jnp.concatenate([jnp.broadcast_to(meta_tokens[None].astype(F32), (bp, N_META, d)), x_prompt], axis=1)
    hp = _pad_rows(hp, FRONT, np_)
    tab_p = _rope_tables(jnp.arange(np_) - FRONT)
    tm_p = _pick_tile(np_, 384)

    s_lo, s_hi = FRONT, FRONT + start + tdec
    ks_len = -(-s_hi // SEQ_ALIGN) * SEQ_ALIGN
    qs0 = ((FRONT + start) // TQ) * TQ
    qoff = FRONT + start - qs0
    assert qoff + tdec <= TQ and qs0 + TQ <= ks_len
    rs = bs * tdec
    hs = x_sample.reshape(1, rs, d)
    tab_s = _rope_tables(jnp.tile(start + jnp.arange(tdec), bs))
    tm_s = _pick_tile(rs, 512)
    unflat = lambda a: a.reshape(bs, tdec, a.shape[-1])
    qtile = lambda a: _pad_rows(unflat(a), qoff, TQ)

    lbp = jax.nn.softmax(d_lb_raw.astype(F32), axis=0)
    lower = jnp.cumsum(lbp, axis=0) - lbp[0]
    zpad = jnp.zeros((d, 88), F32)

    ev_p, ev_s, od_p, od_s = [], [], [], []
    for l in range(depth):
        j = l // 2
        final = l == depth - 1
        g = norm_g[l].astype(F32)[None]
        fg = final_g.astype(F32)[None]
        if l % 2 == 0:
            w = w_in_even[j]
            w = jnp.concatenate([w[:, :936], zpad, w[:, 936:]], axis=1).astype(BF16)
            wo = w_out_even[j].astype(BF16)
            aq, kv, iq, ikw, ag, bq, bk, bv, bg = _proj(hp, g, w, tab_p, EVEN_SEGS, tm_p)
            ao = _dsa(iq, ikw, aq, ikw, kv, q0=0, k_lo=p_lo, k_hi=p_hi, topk=topk_p)
            bo = _sb(bq, bk, bv, q0=0, k_lo=p_lo)
            hp = _outproj(hp, ao, ag, bo, bg, wo, fg, p_lo, p_hi, final, tm_p)
            ev_p.append((kv[:, p_lo:p_hi, :HEAD_DH], kv[:, p_lo:p_hi, HEAD_DH:], ikw[:, p_lo:p_hi, :IDX_DIM],
                         bk[:, p_lo:p_hi], bv[:, p_lo:p_hi]))
            aq, kv, iq, ikw, ag, bq, bk, bv, bg = _proj(hs, g, w, tab_s, EVEN_SEGS, tm_s)
            kv_n, ikw_n, bk_n, bv_n = unflat(kv), unflat(ikw), unflat(bk), unflat(bv)
            kv_c = jnp.concatenate([cache_a_k[j].reshape(bs, start, HEAD_DH),
                                    cache_a_v[j].reshape(bs, start, HEAD_DH)], axis=-1)
            ik_c = jnp.pad(cache_a_ik[j], ((0, 0), (0, 0), (0, LANES - IDX_DIM)))
            full = lambda cch, new: _pad_rows(jnp.concatenate([cch, new], axis=1), FRONT, ks_len)
            ao = _dsa(qtile(iq), qtile(ikw), qtile(aq), full(ik_c, ikw_n), full(kv_c, kv_n),
                      q0=qs0, k_lo=s_lo, k_hi=s_hi, topk=topk_s)
            bo = _sb(qtile(bq), full(cache_b_k[j].reshape(bs, start, -1), bk_n),
                     full(cache_b_v[j].reshape(bs, start, -1), bv_n), q0=qs0, k_lo=s_lo)
            flat = lambda a: a[:, qoff:qoff + tdec].reshape(1, rs, a.shape[-1])
            hs = _outproj(hs, flat(ao), ag, flat(bo), bg, wo, fg, 0, rs, final, tm_s)
            ev_s.append((kv_n[..., :HEAD_DH], kv_n[..., HEAD_DH:], ikw_n[..., :IDX_DIM], bk_n, bv_n))
        else:
            w = w_in_odd[j].astype(BF16)
            wo = w_out_odd[j].astype(BF16)
            lb = lower[l][None]
            gn = jnp.tile(d_norm_g[j].astype(F32), D_HEADS)[None]
            sink = c_sinks[j].astype(F32)
            cq, kv, cg, dq, df, di, dg = _proj(hp, g, w, tab_p, ODD_SEGS, tm_p)
            co = _swa(sink, cq, jnp.pad(kv, ((0, 0), (TQ, 0), (0, 0))), q0=0, k_lo=p_lo, k_hi=p_hi)
            s0 = jnp.zeros((bp, D_HEADS, D_DK, D_DK), F32)
            do, sT = _hgrn(dq, df, di, lb, gn, s0, p0=0, lo=p_lo, hi=p_hi)
            hp = _outproj(hp, co, cg, do, dg, wo, fg, p_lo, p_hi, final, tm_p)
            od_p.append((kv[:, p_hi - window:p_hi, :HEAD_DH], kv[:, p_hi - window:p_hi, HEAD_DH:],
                         jnp.swapaxes(sT, 2, 3)))
            cq, kv, cg, dq, df, di, dg = _proj(hs, g, w, tab_s, ODD_SEGS, tm_s)
            kv_n = unflat(kv)
            kv_c = jnp.concatenate([cache_c_k[j].reshape(bs, window, HEAD_DH),
                                    cache_c_v[j].reshape(bs, window, HEAD_DH)], axis=-1)
            kv_w = jnp.concatenate([kv_c, kv_n], axis=1)
            kv_t = _pad_rows(kv_w, qoff, 2 * TQ)
            co = _swa(sink, qtile(cq), kv_t, q0=qs0, k_lo=s_hi - tdec - window, k_hi=s_hi)
            ctile = lambda a: _pad_rows(unflat(a), 0, CHUNK)
            do, sT = _hgrn(ctile(dq), ctile(df), ctile(di), lb, gn, jnp.swapaxes(state_d[j].astype(F32), 2, 3),
                           p0=0, lo=0, hi=tdec)
            flat = lambda a, o: a[:, o:o + tdec].reshape(1, rs, a.shape[-1])
            hs = _outproj(hs, flat(co, qoff), cg, flat(do, 0), dg, wo, fg, 0, rs, final, tm_s)
            od_s.append((kv_w[:, -window:, :HEAD_DH], kv_w[:, -window:, HEAD_DH:], jnp.swapaxes(sT, 2, 3)))

    y_prompt = hp[:, p_lo + N_META:p_hi]
    y_sample = hs.reshape(bs, tdec, d)
    stk = lambda group, idx, shape: jnp.stack([gp[idx] for gp in group]).reshape((len(group),) + shape)
    a_shape = lambda bb, tt: (bb, tt, 1, HEAD_DH)
    b_shape = lambda bb, tt: (bb, tt, N_HEADS, HEAD_DH)
    return (y_prompt, y_sample,
            stk(ev_p, 0, a_shape(bp, n)), stk(ev_s, 0, a_shape(bs, tdec)),
            stk(ev_p, 1, a_shape(bp, n)), stk(ev_s, 1, a_shape(bs, tdec)),
            stk(ev_p, 2, (bp, n, IDX_DIM)), stk(ev_s, 2, (bs, tdec, IDX_DIM)),
            stk(ev_p, 3, b_shape(bp, n)), stk(ev_s, 3, b_shape(bs, tdec)),
            stk(ev_p, 4, b_shape(bp, n)), stk(ev_s, 4, b_shape(bs, tdec)),
            stk(od_p, 0, a_shape(bp, window)), stk(od_s, 0, a_shape(bs, window)),
            stk(od_p, 1, a_shape(bp, window)), stk(od_s, 1, a_shape(bs, window)),
            stk(od_p, 2, (bp, D_HEADS, D_DK, D_DK)), stk(od_s, 2, (bs, D_HEADS, D_DK, D_DK)))
```

```python
import functools

import numpy as np
import jax
import jax.numpy as jnp
from jax import lax
from jax.experimental import pallas as pl
from jax.experimental.pallas import tpu as pltpu

F32 = jnp.float32
BF16 = jnp.bfloat16
I32 = jnp.int32

CHUNK = 64
N_META = 16
FRONT = CHUNK - N_META
ROPE_THETA = 500000.0
EPS = 1e-6
HEAD_DH = 64
N_HEADS = 8
IDX_HEADS = 8
IDX_DIM = 32
TOPK_MAX = 256
WIN_CHUNKS = 2
D_HEADS = 4
D_DK = 128

LANES = 128
TQ = 128
TK = 256
SB_TK = 256
SEQ_ALIGN = 256
VMEM_LIMIT = 56 * 1024 * 1024

NEG = -1e30
SB_UNDERFLOW = -105.0
INT_MIN = -2 ** 31


def _nt(a, b):
    return lax.dot_general(a, b, (((1,), (1,)), ((), ())), preferred_element_type=F32)


def _mm(a, b):
    return jnp.dot(a, b, preferred_element_type=F32)


def _split3(x):
    hi = x.astype(BF16)
    r1 = x - hi.astype(F32)
    mid = r1.astype(BF16)
    lo = (r1 - mid.astype(F32)).astype(BF16)
    return hi, mid, lo


def _lane_iota(shape):
    return lax.broadcasted_iota(I32, shape, len(shape) - 1)


def _row_iota(shape):
    return lax.broadcasted_iota(I32, shape, len(shape) - 2)


def _rope128(y, c, s1, s2, shift):
    return y * c + pltpu.roll(y, LANES - shift, 1) * s1 + pltpu.roll(y, shift, 1) * s2


def _proj_kernel(x_ref, g_ref, w_ref, tab_ref, *out_refs, segs, iw_scale):
    x = x_ref[...]
    ms = jnp.mean(x * x, axis=-1, keepdims=True)
    xn = (x * lax.rsqrt(ms + EPS) * g_ref[...]).astype(BF16)
    tm = x.shape[0]
    lane = _lane_iota((tm, LANES))
    for (c0, width, kind), o_ref in zip(segs, out_refs):
        y = _mm(xn, w_ref[:, c0:c0 + width])
        if kind == "none":
            o_ref[...] = y
            continue
        if kind in ("ropeA_q", "ropeA_kv"):
            c, s1, s2 = (tab_ref[:, i * LANES:(i + 1) * LANES] for i in (0, 1, 2))
            shift = HEAD_DH // 8
        else:
            c, s1, s2 = (tab_ref[:, i * LANES:(i + 1) * LANES] for i in (3, 4, 5))
            shift = IDX_DIM // 8
        if kind == "ropeA_kv":
            keep = lane < HEAD_DH
            c, s1, s2 = jnp.where(keep, c, 1.0), jnp.where(keep, s1, 0.0), jnp.where(keep, s2, 0.0)
        if kind == "ropeI_kw":
            keep = lane < IDX_DIM
            c = jnp.where(keep, c, jnp.where(lane < IDX_DIM + IDX_HEADS, iw_scale, 1.0))
            s1, s2 = jnp.where(keep, s1, 0.0), jnp.where(keep, s2, 0.0)
        for g in range(width // LANES):
            r = _rope128(y[:, g * LANES:(g + 1) * LANES], c, s1, s2, shift)
            if kind == "ropeA_q":
                r = r * (HEAD_DH ** -0.5)
            o_ref[:, g * LANES:(g + 1) * LANES] = r


def _proj(x, g, w, tab, segs, tm):
    b, t, d = x.shape
    pc = w.shape[1]
    kern = functools.partial(_proj_kernel, segs=segs, iw_scale=float(IDX_DIM ** -0.5 * IDX_HEADS ** -0.5))
    return pl.pallas_call(
        kern,
        grid=(b, t // tm),
        in_specs=[
            pl.BlockSpec((None, tm, d), lambda bi, i: (bi, i, 0)),
            pl.BlockSpec((1, d), lambda bi, i: (0, 0)),
            pl.BlockSpec((d, pc), lambda bi, i: (0, 0)),
            pl.BlockSpec((tm, tab.shape[1]), lambda bi, i: (i, 0)),
        ],
        out_specs=[pl.BlockSpec((None, tm, wd), lambda bi, i: (bi, i, 0)) for (_, wd, _) in segs],
        out_shape=[jax.ShapeDtypeStruct((b, t, wd), F32) for (_, wd, _) in segs],
        compiler_params=pltpu.CompilerParams(
            dimension_semantics=("parallel", "parallel"), vmem_limit_bytes=VMEM_LIMIT),
        name="in_proj",
    )(x, g, w, tab)


EVEN_SEGS = ((0, 512, "ropeA_q"), (512, 128, "ropeA_kv"), (640, 256, "ropeI_q"), (896, 128, "ropeI_kw"),
             (1024, 512, "none"), (1536, 512, "none"), (2048, 512, "none"), (2560, 512, "none"),
             (3072, 512, "none"))
ODD_SEGS = ((0, 512, "ropeA_q"), (512, 128, "ropeA_kv"), (640, 512, "none"), (1152, 512, "none"),
            (1664, 512, "none"), (2176, 512, "none"), (2688, 512, "none"))


def _rope_tables(pos):
    posf = pos.astype(F32)[:, None]
    lane = np.arange(LANES)
    out = []
    for dh in (HEAD_DH, IDX_DIM):
        r = dh // 4
        half = r // 2
        inv = ROPE_THETA ** (-(jnp.arange(half, dtype=F32) * 2.0 / r))
        ang = posf * inv[None, :]
        cos, sin = jnp.cos(ang), jnp.sin(ang)
        jj = lane % dh
        fi = jj % half
        cos_l, sin_l = cos[:, fi], sin[:, fi]
        out.append(jnp.where((jj < r)[None], cos_l, 1.0))
        out.append(jnp.where((jj < half)[None], -sin_l, 0.0))
        out.append(jnp.where(((jj >= half) & (jj < r))[None], sin_l, 0.0))
    return jnp.concatenate(out, axis=1)


def _silu(g):
    return g / (1.0 + jnp.exp(-g))


def _outproj_kernel(h_ref, o1_ref, g1_ref, o2_ref, g2_ref, w_ref, fg_ref, out_ref, *, lo, hi, final):
    tm = h_ref.shape[0]
    half = o1_ref.shape[1]
    m1 = (o1_ref[...] * _silu(g1_ref[...])).astype(BF16)
    m2 = (o2_ref[...] * _silu(g2_ref[...])).astype(BF16)
    hn = h_ref[...] + _mm(m1, w_ref[0:half, :]) + _mm(m2, w_ref[half:2 * half, :])
    if final:
        ms = jnp.mean(hn * hn, axis=-1, keepdims=True)
        hn = hn * lax.rsqrt(ms + EPS) * fg_ref[...]
    row = pl.program_id(1) * tm + _row_iota((tm, 1))
    out_ref[...] = jnp.where((row >= lo) & (row < hi), hn, 0.0)


def _outproj(h, o1, g1, o2, g2, w, fg, lo, hi, final, tm):
    b, t, d = h.shape
    half = o1.shape[2]
    row = lambda wd: pl.BlockSpec((None, tm, wd), lambda bi, i: (bi, i, 0))
    return pl.pallas_call(
        functools.partial(_outproj_kernel, lo=lo, hi=hi, final=final),
        grid=(b, t // tm),
        in_specs=[row(d), row(half), row(half), row(half), row(half),
                  pl.BlockSpec((2 * half, d), lambda bi, i: (0, 0)),
                  pl.BlockSpec((1, d), lambda bi, i: (0, 0))],
        out_specs=row(d),
        out_shape=jax.ShapeDtypeStruct((b, t, d), F32),
        compiler_params=pltpu.CompilerParams(
            dimension_semantics=("parallel", "parallel"), vmem_limit_bytes=VMEM_LIMIT),
        name="out_proj",
    )(h, o1, g1, o2, g2, w, fg)


def _stage_heads(q_ref, qh_ref):
    tq = q_ref.shape[0]
    lane = _lane_iota((tq, LANES))
    for hp in range(N_HEADS // 2):
        pair = q_ref[:, hp * LANES:(hp + 1) * LANES]
        qh_ref[2 * hp] = jnp.where(lane < HEAD_DH, pair, 0.0).astype(BF16)
        qh_ref[2 * hp + 1] = jnp.where(lane < HEAD_DH, pltpu.roll(pair, HEAD_DH, 1), 0.0).astype(BF16)


def _kv_tiles(kvt):
    lane = _lane_iota(kvt.shape)
    vaug = jnp.where(lane < HEAD_DH, pltpu.roll(kvt, HEAD_DH, 1), 1.0)
    return kvt.astype(BF16), vaug.astype(BF16)


def _merge_heads(res, out_ref):
    lane = _lane_iota(res[0].shape)
    for hp in range(N_HEADS // 2):
        out_ref[:, hp * LANES:(hp + 1) * LANES] = jnp.where(
            lane < HEAD_DH, res[2 * hp], pltpu.roll(res[2 * hp + 1], HEAD_DH, 1))


def _fold8(x, op):
    while x.shape[0] > 8:
        half = x.shape[0] // 2
        x = x[:half] + x[half:] if op == "sum" else jnp.maximum(x[:half], x[half:])
    return x


def _dsa_kernel(iq_ref, wq_ref, aq_ref, ik_ref, kv_ref, out_ref,
                sc_ref, iq2_ref, q2_ref, acc_ref, *, q0, k_lo, k_hi, topk):
    tq = iq_ref.shape[0]
    kp = ik_ref.shape[0]
    npair = N_HEADS // 2
    i = pl.program_id(1)
    qbase = q0 + i * tq
    nk = jnp.minimum(kp // TK, (qbase + tq + TK - 1) // TK)
    cq = (qbase + _lane_iota((1, tq))) // CHUNK
    n_adm = jnp.clip((cq + 1) * CHUNK, k_lo, k_hi) - k_lo
    kq = jnp.minimum(topk, n_adm).astype(F32)

    lane = _lane_iota((tq, LANES))
    for g in range(IDX_HEADS * IDX_DIM // LANES):
        grp = iq_ref[:, g * LANES:(g + 1) * LANES]
        for o in range(LANES // IDX_DIM):
            h = g * (LANES // IDX_DIM) + o
            sh = grp if o == 0 else pltpu.roll(grp, LANES - o * IDX_DIM, 1)
            iq2_ref[h // 2, (h % 2) * tq:(h % 2 + 1) * tq, :] = jnp.where(lane < IDX_DIM, sh, 0.0).astype(BF16)
    for p in range(npair):
        pair = aq_ref[:, p * LANES:(p + 1) * LANES]
        q2_ref[p, 0:tq, :] = jnp.where(lane < HEAD_DH, pair, 0.0).astype(BF16)
        q2_ref[p, tq:2 * tq, :] = jnp.where(lane < HEAD_DH, pltpu.roll(pair, HEAD_DH, 1), 0.0).astype(BF16)
    wt = wq_ref[...].T
    wrow = [wt[IDX_DIM + h:IDX_DIM + h + 1, :] for h in range(IDX_HEADS)]

    def score_body(j, carry):
        k0 = pl.multiple_of(j * TK, TK)
        kib = ik_ref[pl.ds(k0, TK), :].astype(BF16)
        tot = jnp.zeros((TK, tq), F32)
        raw = [_nt(kib, iq2_ref[p]) for p in range(npair)]
        for p in range(npair):
            tot = tot + jnp.maximum(raw[p][:, 0:tq], 0.0) * wrow[2 * p]
            tot = tot + jnp.maximum(raw[p][:, tq:2 * tq], 0.0) * wrow[2 * p + 1]
        pk = k0 + _row_iota((TK, tq))
        adm = (pk // CHUNK <= cq) & (pk >= k_lo) & (pk < k_hi)
        bits = pltpu.bitcast(tot, I32)
        key = bits ^ ((bits >> 31) & 0x7FFFFFFF)
        sc_ref[pl.ds(k0, TK), :] = jnp.where(adm, key, INT_MIN)
        return carry
    lax.fori_loop(0, nk, score_body, 0)

    def count_ge(cand):
        def body(j, cnt):
            x = sc_ref[pl.ds(pl.multiple_of(j * TK, TK), TK), :]
            return cnt + _fold8(jnp.where(x >= cand, 1.0, 0.0), "sum")
        cnt = lax.fori_loop(0, nk, body, jnp.zeros((8, tq), F32))
        return jnp.sum(cnt, axis=0, keepdims=True)

    c0 = count_ge(jnp.zeros((1, tq), I32))
    base0 = jnp.where(c0 >= kq, 0, INT_MIN).astype(I32)
    cb0 = jnp.where(c0 >= kq, c0, 2.0 * kp)

    def bit_cond(st):
        bi, _, cb = st
        return (bi < 31) & (jnp.max(jnp.abs(cb - kq)) > 0.0)

    def bit_body(st):
        bi, base, cb = st
        cand = base | jnp.left_shift(jnp.int32(1), 30 - bi)
        c = count_ge(cand)
        ok = c >= kq
        return bi + 1, jnp.where(ok, cand, base), jnp.where(ok, c, cb)
    _, thr, cthr = lax.while_loop(bit_cond, bit_body, (jnp.int32(0), base0, cb0))
    has_tie = jnp.max(cthr - kq) > 0.0

    acc_ref[...] = jnp.zeros(acc_ref.shape, F32)
    rowv = _row_iota((LANES, TK))

    def attend(j, ms, sel):
        k0 = pl.multiple_of(j * TK, TK)
        kvt = kv_ref[pl.ds(k0, TK), :]
        kb = kvt.astype(BF16)
        vaug_t = jnp.where(rowv < HEAD_DH, pltpu.roll(kvt, HEAD_DH, 1).T, 1.0).astype(BF16)
        sel2 = jnp.concatenate([sel, sel], axis=1)
        raw = [_nt(kb, q2_ref[p]) for p in range(npair)]
        out, pexp = [], []
        for p in range(npair):
            s = jnp.where(sel2, raw[p], NEG)
            m_new = jnp.maximum(ms[p], jnp.max(_fold8(s, "max"), axis=0, keepdims=True))
            pexp.append(jnp.exp(s - m_new).astype(BF16))
            out.append(m_new)
        pv = [_mm(vaug_t, pexp[p]) for p in range(npair)]
        for p in range(npair):
            acc_ref[p] = jnp.exp(ms[p] - out[p]) * acc_ref[p] + pv[p]
        return tuple(out)

    ms0 = tuple(jnp.full((1, 2 * tq), NEG, F32) for _ in range(npair))

    def fast_loop():
        def body(j, ms):
            x = sc_ref[pl.ds(pl.multiple_of(j * TK, TK), TK), :]
            return attend(j, ms, x >= thr)
        return lax.fori_loop(0, nk, body, ms0)

    def tie_loop():
        need = kq - count_ge(thr + 1)
        lower = (_lane_iota((TK, TK)) < _row_iota((TK, TK))).astype(BF16)

        def body(j, st):
            ms, run = st
            x = sc_ref[pl.ds(pl.multiple_of(j * TK, TK), TK), :]
            eq = x == thr
            eqf = jnp.where(eq, 1.0, 0.0)
            before = _mm(lower, eqf.astype(BF16)) + run
            sel = (x > thr) | (eq & (before < need))
            run = run + jnp.sum(_fold8(eqf, "sum"), axis=0, keepdims=True)
            return attend(j, ms, sel), run
        ms, _ = lax.fori_loop(0, nk, body, (ms0, jnp.zeros((1, tq), F32)))
        return ms
    lax.cond(has_tie, tie_loop, fast_loop)

    for p in range(npair):
        a = acc_ref[p]
        o2 = jnp.concatenate([a[0:HEAD_DH, 0:tq] / a[HEAD_DH:HEAD_DH + 1, 0:tq],
                              a[0:HEAD_DH, tq:2 * tq] / a[HEAD_DH:HEAD_DH + 1, tq:2 * tq]], axis=0)
        out_ref[:, p * LANES:(p + 1) * LANES] = o2.T


def _dsa(iq, wq, aq, ik, kv, *, q0, k_lo, k_hi, topk):
    b, nq, _ = iq.shape
    kp = ik.shape[1]
    qspec = lambda wd: pl.BlockSpec((None, TQ, wd), lambda bi, i: (bi, i, 0))
    kspec = pl.BlockSpec((None, kp, LANES), lambda bi, i: (bi, 0, 0))
    return pl.pallas_call(
        functools.partial(_dsa_kernel, q0=q0, k_lo=k_lo, k_hi=k_hi, topk=topk),
        grid=(b, nq // TQ),
        in_specs=[qspec(iq.shape[2]), qspec(LANES), qspec(aq.shape[2]), kspec, kspec],
        out_specs=qspec(aq.shape[2]),
        out_shape=jax.ShapeDtypeStruct(aq.shape, F32),
        scratch_shapes=[pltpu.VMEM((kp, TQ), I32),
                        pltpu.VMEM((IDX_HEADS // 2, 2 * TQ, LANES), BF16),
                        pltpu.VMEM((N_HEADS // 2, 2 * TQ, LANES), BF16),
                        pltpu.VMEM((N_HEADS // 2, LANES, 2 * TQ), F32)],
        compiler_params=pltpu.CompilerParams(
            dimension_semantics=("parallel", "parallel"), vmem_limit_bytes=VMEM_LIMIT),
        name="dsa_attn",
    )(iq, wq, aq, ik, kv)


def _sb_kernel(q_ref, k_ref, v_ref, out_ref, *, q0, k_lo):
    tq = q_ref.shape[0]
    i = pl.program_id(2)
    qbase = q0 + i * tq
    pq = qbase + _row_iota((tq, 1))
    tk = SB_TK
    jmax = (qbase + tq - 1) // tk
    lane = _lane_iota((tq, LANES))
    qpair = q_ref[...] * (HEAD_DH ** -0.5)
    qh = [jnp.where(lane < HEAD_DH, qpair, 0.0).astype(BF16),
          jnp.where(lane >= HEAD_DH, qpair, 0.0).astype(BF16)]
    ugt = (_row_iota((tk, tk)) > _lane_iota((tk, tk))).astype(BF16)

    def body(state):
        t, _, carry = state
        j = jmax - t
        k0 = pl.multiple_of(j * tk, tk)
        kb = k_ref[pl.ds(k0, tk), :].astype(BF16)
        vb = v_ref[pl.ds(k0, tk), :].astype(BF16)
        pk = k0 + _lane_iota((tq, tk))
        strict = (pk < pq) & (pk >= k_lo)
        zs = [_nt(qh[h], kb) for h in range(2)]
        lss = [jnp.minimum(-z, 0.0) - jnp.log(1.0 + jnp.exp(-jnp.abs(z))) for z in zs]
        lms = [jnp.where(strict, ls, 0.0) for ls in lss]
        parts = [_split3(lm) for lm in lms]
        prods = [[_mm(x, ugt) for x in part] for part in parts]
        laters = [pr[0] + pr[1] + pr[2] for pr in prods]
        aw = [jnp.exp(jnp.where(strict, zs[h] + lss[h] + laters[h] + carry[h][0], NEG)).astype(BF16)
              for h in range(2)]
        pvs = [_mm(aw[h], vb) for h in range(2)]
        new = [(carry[h][0] + laters[h][:, 0:1] + lms[h][:, 0:1], carry[h][1] + pvs[h]) for h in range(2)]
        alive = jnp.max(jnp.maximum(new[0][0], new[1][0])) >= SB_UNDERFLOW
        return t + 1, alive, tuple(new)

    zero = (jnp.zeros((tq, 1), F32), jnp.zeros((tq, LANES), F32))
    _, _, ((_, acc0), (_, acc1)) = lax.while_loop(
        lambda st: (st[0] <= jmax) & st[1], body, (jnp.int32(0), jnp.bool_(True), (zero, zero)))
    out_ref[...] = jnp.where(lane < HEAD_DH, acc0, acc1)


def _sb(q, k, v, *, q0, k_lo):
    b, nq, w = q.shape
    kp = k.shape[1]
    return pl.pallas_call(
        functools.partial(_sb_kernel, q0=q0, k_lo=k_lo),
        grid=(b, w // LANES, nq // TQ),
        in_specs=[pl.BlockSpec((None, TQ, LANES), lambda bi, hp, i: (bi, i, hp)),
                  pl.BlockSpec((None, kp, LANES), lambda bi, hp, i: (bi, 0, hp)),
                  pl.BlockSpec((None, kp, LANES), lambda bi, hp, i: (bi, 0, hp))],
        out_specs=pl.BlockSpec((None, TQ, LANES), lambda bi, hp, i: (bi, i, hp)),
        out_shape=jax.ShapeDtypeStruct(q.shape, F32),
        compiler_params=pltpu.CompilerParams(
            dimension_semantics=("parallel", "parallel", "parallel"), vmem_limit_bytes=VMEM_LIMIT),
        name="sb_attn",
    )(q, k, v)


def _swa_kernel(sink_ref, q_ref, kv0_ref, kv1_ref, out_ref, qh_ref, *, q0, k_lo, k_hi):
    tq = q_ref.shape[0]
    i = pl.program_id(1)
    qbase = q0 + i * tq
    cq = (qbase + _row_iota((tq, 1))) // CHUNK
    pk = qbase + _lane_iota((tq, 2 * tq))
    ck = pk // CHUNK - tq // CHUNK
    mask = (ck <= cq) & (ck >= cq - WIN_CHUNKS) & (pk >= k_lo + tq) & (pk < k_hi + tq)
    _stage_heads(q_ref, qh_ref)
    kb0, va0 = _kv_tiles(kv0_ref[...])
    kb1, va1 = _kv_tiles(kv1_ref[...])
    kb = jnp.concatenate([kb0, kb1], axis=0)
    va = jnp.concatenate([va0, va1], axis=0)
    raw = [_nt(qh_ref[h], kb) for h in range(N_HEADS)]
    ms, ps = [], []
    for h in range(N_HEADS):
        s = jnp.where(mask, raw[h], NEG)
        m = jnp.maximum(jnp.max(s, axis=1, keepdims=True), sink_ref[h])
        ms.append(m)
        ps.append(jnp.exp(s - m).astype(BF16))
    pvs = [_mm(ps[h], va) for h in range(N_HEADS)]
    res = []
    for h in range(N_HEADS):
        den = pvs[h][:, HEAD_DH:HEAD_DH + 1] + jnp.exp(sink_ref[h] - ms[h])
        res.append(pvs[h] / jnp.broadcast_to(den, (tq, LANES)))
    _merge_heads(res, out_ref)


def _swa(sink, q, kv, *, q0, k_lo, k_hi):
    b, nq, w = q.shape
    return pl.pallas_call(
        functools.partial(_swa_kernel, q0=q0, k_lo=k_lo, k_hi=k_hi),
        grid=(b, nq // TQ),
        in_specs=[pl.BlockSpec(memory_space=pltpu.SMEM),
                  pl.BlockSpec((None, TQ, w), lambda bi, i: (bi, i, 0)),
                  pl.BlockSpec((None, TQ, LANES), lambda bi, i: (bi, i, 0)),
                  pl.BlockSpec((None, TQ, LANES), lambda bi, i: (bi, i + 1, 0))],
        out_specs=pl.BlockSpec((None, TQ, w), lambda bi, i: (bi, i, 0)),
        out_shape=jax.ShapeDtypeStruct(q.shape, F32),
        scratch_shapes=[pltpu.VMEM((N_HEADS, TQ, LANES), BF16)],
        compiler_params=pltpu.CompilerParams(
            dimension_semantics=("parallel", "parallel"), vmem_limit_bytes=VMEM_LIMIT),
        name="swa_attn",
    )(sink, q, kv, kv)


HG_LEVELS = (32, 16, 8, 4, 2, 1)
HG_BB = 8


def _hgrn_consts():
    t = np.arange(CHUNK)
    tri = (t[None, :] <= t[:, None]).astype(np.float32)
    rows = [tri]
    for m in HG_LEVELS:
        ref = (t // (2 * m)) * (2 * m) + m - 1
        rows.append(tri[ref])
    return np.concatenate(rows, axis=0)


def _hgrn_kernel(q_ref, f_ref, i_ref, lb_ref, gn_ref, gm_ref, s0_ref, o_ref, sT_ref, st_ref, *, p0, lo, hi):
    c = pl.program_id(1)
    nb = q_ref.shape[0]

    @pl.when(c == 0)
    def _():
        st_ref[...] = s0_ref[...]

    row = p0 + c * CHUNK + _row_iota((CHUNK, 1))
    valid = (row >= lo) & (row < hi)
    tt = _row_iota((CHUNK, CHUNK))
    ss = _lane_iota((CHUNK, CHUNK))
    lmask = [(tt // (2 * m) == ss // (2 * m)) & (tt % (2 * m) >= m) & (ss % (2 * m) < m) for m in HG_LEVELS]
    gmat = gm_ref[...]

    def per_batch(b, carry):
        heads = range(D_HEADS)
        cs = [slice(h * D_DK, (h + 1) * D_DK) for h in heads]
        q, kk, iv, parts = [], [], [], []
        for h in heads:
            lb = lb_ref[:, cs[h]]
            x = f_ref[b, :, cs[h]]
            e = jnp.exp(-jnp.abs(x))
            r = 1.0 / (1.0 + e)
            sig = jnp.where(x >= 0, r, e * r)
            nsig = jnp.where(x >= 0, e * r, r)
            parts.append(_split3(jnp.where(valid, jnp.log(lb + (1.0 - lb) * sig), 0.0)))
            kk.append(jnp.where(valid, (1.0 - lb) * nsig, 0.0))
            q.append(jnp.where(valid, q_ref[b, :, cs[h]], 0.0))
            iv.append(jnp.where(valid, i_ref[b, :, cs[h]], 0.0))
        prods = [[_mm(gmat, x) for x in parts[h]] for h in heads]
        cums = [pr[0] + pr[1] + pr[2] for pr in prods]
        cum = [c[0:CHUNK] for c in cums]
        last = [c[CHUNK - 1:CHUNK] for c in cums]
        qk = []
        for h in heads:
            for li in range(len(HG_LEVELS)):
                ref = cums[h][(li + 1) * CHUNK:(li + 2) * CHUNK]
                qk.append(((q[h] * jnp.exp(jnp.minimum(cum[h] - ref, 0.0))).astype(BF16),
                           (kk[h] * jnp.exp(jnp.minimum(ref - cum[h], 0.0))).astype(BF16)))
        lv = [_nt(a, c) for a, c in qk]
        nl = len(HG_LEVELS)
        att = []
        for h in heads:
            t = jnp.zeros((CHUNK, CHUNK), F32)
            for li in range(nl):
                t = t + jnp.where(lmask[li], lv[h * nl + li], 0.0)
            att.append(t.astype(BF16))
        ivb = [x.astype(BF16) for x in iv]
        stT = [st_ref[b, h] for h in heads]
        o_inter = [_nt((q[h] * jnp.exp(cum[h])).astype(BF16), stT[h].astype(BF16)) for h in heads]
        o_intra = [_mm(att[h], ivb[h]) for h in heads]
        upd = [_mm(iv[h].T.astype(BF16), (kk[h] * jnp.exp(last[h] - cum[h])).astype(BF16)) for h in heads]
        for h in heads:
            st_ref[b, h] = stT[h] * jnp.exp(last[h]) + upd[h]
            o = o_inter[h] + o_intra[h] + jnp.sum(q[h] * kk[h], axis=1, keepdims=True) * iv[h]
            ms = jnp.mean(o * o, axis=1, keepdims=True)
            o_ref[b, :, cs[h]] = o * lax.rsqrt(ms + EPS) * gn_ref[:, cs[h]]
        return carry
    lax.fori_loop(0, nb, per_batch, 0)

    @pl.when(c == pl.num_programs(1) - 1)
    def _():
        sT_ref[...] = st_ref[...]


def _hgrn(dq, df, di, lb, gn, s0T, *, p0, lo, hi):
    b, t, w = dq.shape
    bb = min(HG_BB, b)
    gm = jnp.asarray(_hgrn_consts(), BF16)
    row = pl.BlockSpec((bb, CHUNK, w), lambda bi, c: (bi, c, 0))
    vec = pl.BlockSpec((1, w), lambda bi, c: (0, 0))
    sspec = pl.BlockSpec((bb,) + s0T.shape[1:], lambda bi, c: (bi, 0, 0, 0))
    return pl.pallas_call(
        functools.partial(_hgrn_kernel, p0=p0, lo=lo, hi=hi),
        grid=(b // bb, t // CHUNK),
        in_specs=[row, row, row, vec, vec, pl.BlockSpec(gm.shape, lambda bi, c: (0, 0)), sspec],
        out_specs=[row, sspec],
        out_shape=[jax.ShapeDtypeStruct(dq.shape, F32), jax.ShapeDtypeStruct(s0T.shape, F32)],
        scratch_shapes=[pltpu.VMEM((bb,) + s0T.shape[1:], F32)],
        compiler_params=pltpu.CompilerParams(
            dimension_semantics=("parallel", "arbitrary"), vmem_limit_bytes=VMEM_LIMIT),
        name="hgrn2",
    )(dq, df, di, lb, gn, gm, s0T)


def _pick_tile(t, cap):
    best = 8
    for d in range(8, cap + 1, 8):
        if t % d == 0:
            best = d
    return best


def _pad_rows(x, front, total):
    return jnp.pad(x, ((0, 0), (front, total - front - x.shape[1]), (0, 0)))


def kernel(x_prompt, x_sample, cache_a_k, cache_a_v, cache_a_ik, cache_b_k, cache_b_v, cache_c_k, cache_c_v, state_d, meta_tokens, norm_g, final_g, w_in_even, w_out_even, w_in_odd, w_out_odd, c_sinks, d_lb_raw, d_norm_g):
    bp, seq, d = x_prompt.shape
    bs, tdec, _ = x_sample.shape
    depth = norm_g.shape[0]
    window = cache_c_k.shape[2]
    past = cache_a_k.shape[2] - N_META
    n = N_META + seq
    start = N_META + past
    topk_p = min(TOPK_MAX, seq // 4)
    topk_s = min(TOPK_MAX, (past + tdec) // 4)
    assert window == WIN_CHUNKS * CHUNK and TQ == WIN_CHUNKS * CHUNK

    np_ = -(-(FRONT + n) // SEQ_ALIGN) * SEQ_ALIGN
    p_lo, p_hi = FRONT, FRONT + n
    hp = jnp.concatenate([jnp.broadcast_to(meta_tokens[None].astype(F32), (bp, N_META, d)), x_prompt], axis=1)
    hp = _pad_rows(hp, FRONT, np_)
    tab_p = _rope_tables(jnp.arange(np_) - FRONT)
    tm_p = _pick_tile(np_, 384)

    s_lo, s_hi = FRONT, FRONT + start + tdec
    ks_len = -(-s_hi // SEQ_ALIGN) * SEQ_ALIGN
    qs0 = ((FRONT + start) // TQ) * TQ
    qoff = FRONT + start - qs0
    assert qoff + tdec <= TQ and qs0 + TQ <= ks_len
    rs = bs * tdec
    hs = x_sample.reshape(1, rs, d)
    tab_s = _rope_tables(jnp.tile(start + jnp.arange(tdec), bs))
    tm_s = _pick_tile(rs, 512)
    unflat = lambda a: a.reshape(bs, tdec, a.shape[-1])
    qtile = lambda a: _pad_rows(unflat(a), qoff, TQ)

    lbp = jax.nn.softmax(d_lb_raw.astype(F32), axis=0)
    lower = jnp.cumsum(lbp, axis=0) - lbp[0]
    zpad = jnp.zeros((d, 88), F32)

    ev_p, ev_s, od_p, od_s = [], [], [], []
    for l in range(depth):
        j = l // 2
        final = l == depth - 1
        g = norm_g[l].astype(F32)[None]
        fg = final_g.astype(F32)[None]
        if l % 2 == 0:
            w = w_in_even[j]
            w = jnp.concatenate([w[:, :936], zpad, w[:, 936:]], axis=1).astype(BF16)
            wo = w_out_even[j].astype(BF16)
            aq, kv, iq, ikw, ag, bq, bk, bv, bg = _proj(hp, g, w, tab_p, EVEN_SEGS, tm_p)
            ao = _dsa(iq, ikw, aq, ikw, kv, q0=0, k_lo=p_lo, k_hi=p_hi, topk=topk_p)
            bo = _sb(bq, bk, bv, q0=0, k_lo=p_lo)
            hp = _outproj(hp, ao, ag, bo, bg, wo, fg, p_lo, p_hi, final, tm_p)
            ev_p.append((kv[:, p_lo:p_hi, :HEAD_DH], kv[:, p_lo:p_hi, HEAD_DH:], ikw[:, p_lo:p_hi, :IDX_DIM],
                         bk[:, p_lo:p_hi], bv[:, p_lo:p_hi]))
            aq, kv, iq, ikw, ag, bq, bk, bv, bg = _proj(hs, g, w, tab_s, EVEN_SEGS, tm_s)
            kv_n, ikw_n, bk_n, bv_n = unflat(kv), unflat(ikw), unflat(bk), unflat(bv)
            kv_c = jnp.concatenate([cache_a_k[j].reshape(bs, start, HEAD_DH),
                                    cache_a_v[j].reshape(bs, start, HEAD_DH)], axis=-1)
            ik_c = jnp.pad(cache_a_ik[j], ((0, 0), (0, 0), (0, LANES - IDX_DIM)))
            full = lambda cch, new: _pad_rows(jnp.concatenate([cch, new], axis=1), FRONT, ks_len)
            ao = _dsa(qtile(iq), qtile(ikw), qtile(aq), full(ik_c, ikw_n), full(kv_c, kv_n),
                      q0=qs0, k_lo=s_lo, k_hi=s_hi, topk=topk_s)
            bo = _sb(qtile(bq), full(cache_b_k[j].reshape(bs, start, -1), bk_n),
                     full(cache_b_v[j].reshape(bs, start, -1), bv_n), q0=qs0, k_lo=s_lo)
            flat = lambda a: a[:, qoff:qoff + tdec].reshape(1, rs, a.shape[-1])
            hs = _outproj(hs, flat(ao), ag, flat(bo), bg, wo, fg, 0, rs, final, tm_s)
            ev_s.append((kv_n[..., :HEAD_DH], kv_n[..., HEAD_DH:], ikw_n[..., :IDX_DIM], bk_n, bv_n))
        else:
            w = w_in_odd[j].astype(BF16)
            wo = w_out_odd[j].astype(BF16)
            lb = lower[l][None]
            gn = jnp.tile(d_norm_g[j].astype(F32), D_HEADS)[None]
            sink = c_sinks[j].astype(F32)
            cq, kv, cg, dq, df, di, dg = _proj(hp, g, w, tab_p, ODD_SEGS, tm_p)
            co = _swa(sink, cq, jnp.pad(kv, ((0, 0), (TQ, 0), (0, 0))), q0=0, k_lo=p_lo, k_hi=p_hi)
            s0 = jnp.zeros((bp, D_HEADS, D_DK, D_DK), F32)
            do, sT = _hgrn(dq, df, di, lb, gn, s0, p0=0, lo=p_lo, hi=p_hi)
            hp = _outproj(hp, co, cg, do, dg, wo, fg, p_lo, p_hi, final, tm_p)
            od_p.append((kv[:, p_hi - window:p_hi, :HEAD_DH], kv[:, p_hi - window:p_hi, HEAD_DH:],
                         jnp.swapaxes(sT, 2, 3)))
            cq, kv, cg, dq, df, di, dg = _proj(hs, g, w, tab_s, ODD_SEGS, tm_s)
            kv_n = unflat(kv)
            kv_c = jnp.concatenate([cache_c_k[j].reshape(bs, window, HEAD_DH),
                                    cache_c_v[j].reshape(bs, window, HEAD_DH)], axis=-1)
            kv_w = jnp.concatenate([kv_c, kv_n], axis=1)
            kv_t = _pad_rows(kv_w, qoff, 2 * TQ)
            co = _swa(sink, qtile(cq), kv_t, q0=qs0, k_lo=s_hi - tdec - window, k_hi=s_hi)
            ctile = lambda a: _pad_rows(unflat(a), 0, CHUNK)
            do, sT = _hgrn(ctile(dq), ctile(df), ctile(di), lb, gn, jnp.swapaxes(state_d[j].astype(F32), 2, 3),
                           p0=0, lo=0, hi=tdec)
            flat = lambda a, o: a[:, o:o + tdec].reshape(1, rs, a.shape[-1])
            hs = _outproj(hs, flat(co, qoff), cg, flat(do, 0), dg, wo, fg, 0, rs, final, tm_s)
            od_s.append((kv_w[:, -window:, :HEAD_DH], kv_w[:, -window:, HEAD_DH:], jnp.swapaxes(sT, 2, 3)))

    y_prompt = hp[:, p_lo + N_META:p_hi]
    y_sample = hs.reshape(bs, tdec, d)
    stk = lambda group, idx, shape: jnp.stack([gp[idx] for gp in group]).reshape((len(group),) + shape)
    a_shape = lambda bb, tt: (bb, tt, 1, HEAD_DH)
    b_shape = lambda bb, tt: (bb, tt, N_HEADS, HEAD_DH)
    return (y_prompt, y_sample,
            stk(ev_p, 0, a_shape(bp, n)), stk(ev_s, 0, a_shape(bs, tdec)),
            stk(ev_p, 1, a_shape(bp, n)), stk(ev_s, 1, a_shape(bs, tdec)),
            stk(ev_p, 2, (bp, n, IDX_DIM)), stk(ev_s, 2, (bs, tdec, IDX_DIM)),
            stk(ev_p, 3, b_shape(bp, n)), stk(ev_s, 3, b_shape(bs, tdec)),
            stk(ev_p, 4, b_shape(bp, n)), stk(ev_s, 4, b_shape(bs, tdec)),
            stk(od_p, 0, a_shape(bp, window)), stk(od_s, 0, a_shape(bs, window)),
            stk(od_p, 1, a_shape(bp, window)), stk(od_s, 1, a_shape(bs, window)),
            stk(od_p, 2, (bp, D_HEADS, D_DK, D_DK)), stk(od_s, 2, (bs, D_HEADS, D_DK, D_DK)))
```

```python
import functools

import numpy as np
import jax
import jax.numpy as jnp
from jax import lax
from jax.experimental import pallas as pl
from jax.experimental.pallas import tpu as pltpu

F32 = jnp.float32
BF16 = jnp.bfloat16
I32 = jnp.int32

CHUNK = 64
N_META = 16
FRONT = CHUNK - N_META
ROPE_THETA = 500000.0
EPS = 1e-6
HEAD_DH = 64
N_HEADS = 8
IDX_HEADS = 8
IDX_DIM = 32
TOPK_MAX = 256
WIN_CHUNKS = 2
D_HEADS = 4
D_DK = 128

LANES = 128
TQ = 128
DSA_TQ = 256
TK = 256
SB_TK = 256
SEQ_ALIGN = 256
VMEM_LIMIT = 56 * 1024 * 1024

NEG = -1e30
SB_UNDERFLOW = -105.0
INT_MIN = -2 ** 31
INT_MAX = 2 ** 31 - 1


def _nt(a, b):
    return lax.dot_general(a, b, (((1,), (1,)), ((), ())), preferred_element_type=F32)


def _mm(a, b):
    return jnp.dot(a, b, preferred_element_type=F32)


def _split3(x):
    hi = x.astype(BF16)
    r1 = x - hi.astype(F32)
    mid = r1.astype(BF16)
    lo = (r1 - mid.astype(F32)).astype(BF16)
    return hi, mid, lo


def _lane_iota(shape):
    return lax.broadcasted_iota(I32, shape, len(shape) - 1)


def _row_iota(shape):
    return lax.broadcasted_iota(I32, shape, len(shape) - 2)


def _rope128(y, c, s1, s2, shift):
    return y * c + pltpu.roll(y, LANES - shift, 1) * s1 + pltpu.roll(y, shift, 1) * s2


def _proj_kernel(x_ref, g_ref, w_ref, tab_ref, *out_refs, segs, iw_scale):
    x = x_ref[...]
    ms = jnp.mean(x * x, axis=-1, keepdims=True)
    xn = (x * lax.rsqrt(ms + EPS) * g_ref[...]).astype(BF16)
    tm = x.shape[0]
    lane = _lane_iota((tm, LANES))
    outs = iter(out_refs)
    for c0, width, kind in segs:
        o_ref = next(outs)
        y = _mm(xn, w_ref[:, c0:c0 + width])
        if kind == "none":
            o_ref[...] = y
            continue
        if kind == "dual":
            o_ref[...] = y
            next(outs)[...] = y.astype(BF16)
            continue
        if kind in ("ropeA_q", "ropeA_kv"):
            c, s1, s2 = (tab_ref[:, i * LANES:(i + 1) * LANES] for i in (0, 1, 2))
            shift = HEAD_DH // 8
        else:
            c, s1, s2 = (tab_ref[:, i * LANES:(i + 1) * LANES] for i in (3, 4, 5))
            shift = IDX_DIM // 8
        if kind == "ropeA_kv":
            keep = lane < HEAD_DH
            c, s1, s2 = jnp.where(keep, c, 1.0), jnp.where(keep, s1, 0.0), jnp.where(keep, s2, 0.0)
        if kind == "ropeI_kw":
            keep = lane < IDX_DIM
            c = jnp.where(keep, c, jnp.where(lane < IDX_DIM + IDX_HEADS, iw_scale, 1.0))
            s1, s2 = jnp.where(keep, s1, 0.0), jnp.where(keep, s2, 0.0)
        for g in range(width // LANES):
            r = _rope128(y[:, g * LANES:(g + 1) * LANES], c, s1, s2, shift)
            if kind == "ropeA_q":
                r = r * (HEAD_DH ** -0.5)
            o_ref[:, g * LANES:(g + 1) * LANES] = r


def _proj(x, g, w, tab, segs, tm):
    b, t, d = x.shape
    pc = w.shape[1]
    kern = functools.partial(_proj_kernel, segs=segs, iw_scale=float(IDX_DIM ** -0.5 * IDX_HEADS ** -0.5))
    outs = []
    for _, wd, kind in segs:
        outs += [(wd, F32), (wd, BF16)] if kind == "dual" else [(wd, F32)]
    return pl.pallas_call(
        kern,
        grid=(b, t // tm),
        in_specs=[
            pl.BlockSpec((None, tm, d), lambda bi, i: (bi, i, 0)),
            pl.BlockSpec((1, d), lambda bi, i: (0, 0)),
            pl.BlockSpec((d, pc), lambda bi, i: (0, 0)),
            pl.BlockSpec((tm, tab.shape[1]), lambda bi, i: (i, 0)),
        ],
        out_specs=[pl.BlockSpec((None, tm, wd), lambda bi, i: (bi, i, 0)) for wd, _ in outs],
        out_shape=[jax.ShapeDtypeStruct((b, t, wd), dt) for wd, dt in outs],
        compiler_params=pltpu.CompilerParams(
            dimension_semantics=("parallel", "parallel"), vmem_limit_bytes=VMEM_LIMIT),
        name="in_proj",
    )(x, g, w, tab)


EVEN_SEGS = ((0, 512, "ropeA_q"), (512, 128, "ropeA_kv"), (640, 256, "ropeI_q"), (896, 128, "ropeI_kw"),
             (1024, 512, "none"), (1536, 512, "none"), (2048, 512, "dual"), (2560, 512, "dual"),
             (3072, 512, "none"))
ODD_SEGS = ((0, 512, "ropeA_q"), (512, 128, "ropeA_kv"), (640, 512, "none"), (1152, 512, "none"),
            (1664, 512, "none"), (2176, 512, "none"), (2688, 512, "none"))


def _rope_tables(pos):
    posf = pos.astype(F32)[:, None]
    lane = np.arange(LANES)
    out = []
    for dh in (HEAD_DH, IDX_DIM):
        r = dh // 4
        half = r // 2
        inv = ROPE_THETA ** (-(jnp.arange(half, dtype=F32) * 2.0 / r))
        ang = posf * inv[None, :]
        cos, sin = jnp.cos(ang), jnp.sin(ang)
        jj = lane % dh
        fi = jj % half
        cos_l, sin_l = cos[:, fi], sin[:, fi]
        out.append(jnp.where((jj < r)[None], cos_l, 1.0))
        out.append(jnp.where((jj < half)[None], -sin_l, 0.0))
        out.append(jnp.where(((jj >= half) & (jj < r))[None], sin_l, 0.0))
    return jnp.concatenate(out, axis=1)


def _silu(g):
    return g / (1.0 + jnp.exp(-g))


def _outproj_kernel(h_ref, o1_ref, g1_ref, o2_ref, g2_ref, w_ref, fg_ref, out_ref, *, lo, hi, final):
    tm = h_ref.shape[0]
    half = o1_ref.shape[1]
    m1 = (o1_ref[...] * _silu(g1_ref[...])).astype(BF16)
    m2 = (o2_ref[...] * _silu(g2_ref[...])).astype(BF16)
    hn = h_ref[...] + _mm(m1, w_ref[0:half, :]) + _mm(m2, w_ref[half:2 * half, :])
    if final:
        ms = jnp.mean(hn * hn, axis=-1, keepdims=True)
        hn = hn * lax.rsqrt(ms + EPS) * fg_ref[...]
    row = pl.program_id(1) * tm + _row_iota((tm, 1))
    out_ref[...] = jnp.where((row >= lo) & (row < hi), hn, 0.0)


def _outproj(h, o1, g1, o2, g2, w, fg, lo, hi, final, tm):
    b, t, d = h.shape
    half = o1.shape[2]
    row = lambda wd: pl.BlockSpec((None, tm, wd), lambda bi, i: (bi, i, 0))
    return pl.pallas_call(
        functools.partial(_outproj_kernel, lo=lo, hi=hi, final=final),
        grid=(b, t // tm),
        in_specs=[row(d), row(half), row(half), row(half), row(half),
                  pl.BlockSpec((2 * half, d), lambda bi, i: (0, 0)),
                  pl.BlockSpec((1, d), lambda bi, i: (0, 0))],
        out_specs=row(d),
        out_shape=jax.ShapeDtypeStruct((b, t, d), F32),
        compiler_params=pltpu.CompilerParams(
            dimension_semantics=("parallel", "parallel"), vmem_limit_bytes=VMEM_LIMIT),
        name="out_proj",
    )(h, o1, g1, o2, g2, w, fg)


def _stage_heads(q_ref, qh_ref):
    tq = q_ref.shape[0]
    lane = _lane_iota((tq, LANES))
    for hp in range(N_HEADS // 2):
        pair = q_ref[:, hp * LANES:(hp + 1) * LANES]
        qh_ref[2 * hp] = jnp.where(lane < HEAD_DH, pair, 0.0).astype(BF16)
        qh_ref[2 * hp + 1] = jnp.where(lane < HEAD_DH, pltpu.roll(pair, HEAD_DH, 1), 0.0).astype(BF16)


def _kv_tiles(kvt):
    lane = _lane_iota(kvt.shape)
    vaug = jnp.where(lane < HEAD_DH, pltpu.roll(kvt, HEAD_DH, 1), 1.0)
    return kvt.astype(BF16), vaug.astype(BF16)


def _merge_heads(res, out_ref):
    lane = _lane_iota(res[0].shape)
    for hp in range(N_HEADS // 2):
        out_ref[:, hp * LANES:(hp + 1) * LANES] = jnp.where(
            lane < HEAD_DH, res[2 * hp], pltpu.roll(res[2 * hp + 1], HEAD_DH, 1))


def _fold8(x, op):
    while x.shape[0] > 8:
        half = x.shape[0] // 2
        a, b = x[:half], x[half:]
        x = a + b if op == "sum" else (jnp.maximum(a, b) if op == "max" else jnp.minimum(a, b))
    return x


def _dsa_kernel(iq_ref, wq_ref, aq_ref, ik_ref, kv_ref, out_ref,
                sc_ref, iq2_ref, q2_ref, acc_ref, *, q0, k_lo, k_hi, topk):
    tq = iq_ref.shape[0]
    kp = ik_ref.shape[0]
    npair = N_HEADS // 2
    i = pl.program_id(1)
    qbase = q0 + i * tq
    nk = jnp.minimum(kp // TK, (qbase + tq + TK - 1) // TK)
    cq = (qbase + _lane_iota((1, tq))) // CHUNK
    n_adm = jnp.clip((cq + 1) * CHUNK, k_lo, k_hi) - k_lo
    kq = jnp.minimum(topk, n_adm).astype(F32)

    lane = _lane_iota((tq, LANES))
    for g in range(IDX_HEADS * IDX_DIM // LANES):
        grp = iq_ref[:, g * LANES:(g + 1) * LANES]
        for o in range(LANES // IDX_DIM):
            h = g * (LANES // IDX_DIM) + o
            sh = grp if o == 0 else pltpu.roll(grp, LANES - o * IDX_DIM, 1)
            iq2_ref[h // 2, (h % 2) * tq:(h % 2 + 1) * tq, :] = jnp.where(lane < IDX_DIM, sh, 0.0).astype(BF16)
    for p in range(npair):
        pair = aq_ref[:, p * LANES:(p + 1) * LANES]
        q2_ref[p, 0:tq, :] = jnp.where(lane < HEAD_DH, pair, 0.0).astype(BF16)
        q2_ref[p, tq:2 * tq, :] = jnp.where(lane < HEAD_DH, pltpu.roll(pair, HEAD_DH, 1), 0.0).astype(BF16)
    wt = wq_ref[...].T
    wrow = [wt[IDX_DIM + h:IDX_DIM + h + 1, :] for h in range(IDX_HEADS)]

    def score_body(j, carry):
        k0 = pl.multiple_of(j * TK, TK)
        kib = ik_ref[pl.ds(k0, TK), :].astype(BF16)
        tot = jnp.zeros((TK, tq), F32)
        raw = [_nt(kib, iq2_ref[p]) for p in range(npair)]
        for p in range(npair):
            tot = tot + jnp.maximum(raw[p][:, 0:tq], 0.0) * wrow[2 * p]
            tot = tot + jnp.maximum(raw[p][:, tq:2 * tq], 0.0) * wrow[2 * p + 1]
        pk = k0 + _row_iota((TK, tq))
        adm = (pk // CHUNK <= cq) & (pk >= k_lo) & (pk < k_hi)
        bits = pltpu.bitcast(tot, I32)
        key = bits ^ ((bits >> 31) & 0x7FFFFFFF)
        sc_ref[pl.ds(k0, TK), :] = jnp.where(adm, key, INT_MIN)
        return carry
    lax.fori_loop(0, nk, score_body, 0)

    def count_ge(cand):
        def body(j, cnt):
            x = sc_ref[pl.ds(pl.multiple_of(j * TK, TK), TK), :]
            return cnt + _fold8(jnp.where(x >= cand, 1.0, 0.0), "sum")
        cnt = lax.fori_loop(0, nk, body, jnp.zeros((8, tq), F32))
        return jnp.sum(cnt, axis=0, keepdims=True)

    c0 = count_ge(jnp.zeros((1, tq), I32))
    base0 = jnp.where(c0 >= kq, 0, INT_MIN).astype(I32)
    cb0 = jnp.where(c0 >= kq, c0, 2.0 * kp)

    def bit_cond(st):
        bi, _, cb = st
        return (bi < 31) & (jnp.max(jnp.abs(cb - kq)) > 0.0)

    def bit_body(st):
        bi, base, cb = st
        cand = base | jnp.left_shift(jnp.int32(1), 30 - bi)
        c = count_ge(cand)
        ok = c >= kq
        return bi + 1, jnp.where(ok, cand, base), jnp.where(ok, c, cb)
    _, thr, cthr = lax.while_loop(bit_cond, bit_body, (jnp.int32(0), base0, cb0))
    has_tie = jnp.max(cthr - kq) > 0.0

    acc_ref[...] = jnp.zeros(acc_ref.shape, F32)
    rowv = _row_iota((LANES, TK))

    def attend(j, ms, sel):
        k0 = pl.multiple_of(j * TK, TK)
        kvt = kv_ref[pl.ds(k0, TK), :]
        kb = kvt.astype(BF16)
        vaug_t = jnp.where(rowv < HEAD_DH, pltpu.roll(kvt, HEAD_DH, 1).T, 1.0).astype(BF16)
        sel2 = jnp.concatenate([sel, sel], axis=1)
        raw = [_nt(kb, q2_ref[p]) for p in range(npair)]
        out, pexp = [], []
        for p in range(npair):
            s = jnp.where(sel2, raw[p], NEG)
            m_new = jnp.maximum(ms[p], jnp.max(_fold8(s, "max"), axis=0, keepdims=True))
            pexp.append(jnp.exp(s - m_new).astype(BF16))
            out.append(m_new)
        pv = [_mm(vaug_t, pexp[p]) for p in range(npair)]
        for p in range(npair):
            acc_ref[p] = jnp.exp(ms[p] - out[p]) * acc_ref[p] + pv[p]
        return tuple(out)

    ms0 = tuple(jnp.full((1, 2 * tq), NEG, F32) for _ in range(npair))

    def fast_loop():
        def body(j, ms):
            x = sc_ref[pl.ds(pl.multiple_of(j * TK, TK), TK), :]
            return attend(j, ms, x >= thr)
        return lax.fori_loop(0, nk, body, ms0)

    def tie_loop():
        need = kq - count_ge(thr + 1)
        lower = (_lane_iota((TK, TK)) < _row_iota((TK, TK))).astype(BF16)

        def body(j, st):
            ms, run = st
            x = sc_ref[pl.ds(pl.multiple_of(j * TK, TK), TK), :]
            eq = x == thr
            eqf = jnp.where(eq, 1.0, 0.0)
            before = _mm(lower, eqf.astype(BF16)) + run
            sel = (x > thr) | (eq & (before < need))
            run = run + jnp.sum(_fold8(eqf, "sum"), axis=0, keepdims=True)
            return attend(j, ms, sel), run
        ms, _ = lax.fori_loop(0, nk, body, (ms0, jnp.zeros((1, tq), F32)))
        return ms
    lax.cond(has_tie, tie_loop, fast_loop)

    for p in range(npair):
        a = acc_ref[p]
        o2 = jnp.concatenate([a[0:HEAD_DH, 0:tq] / a[HEAD_DH:HEAD_DH + 1, 0:tq],
                              a[0:HEAD_DH, tq:2 * tq] / a[HEAD_DH:HEAD_DH + 1, tq:2 * tq]], axis=0)
        out_ref[:, p * LANES:(p + 1) * LANES] = o2.T


def _dsa(iq, wq, aq, ik, kv, *, q0, k_lo, k_hi, topk):
    b, nq, _ = iq.shape
    kp = ik.shape[1]
    tq = DSA_TQ
    qspec = lambda wd: pl.BlockSpec((None, tq, wd), lambda bi, i: (bi, i, 0))
    kspec = pl.BlockSpec((None, kp, LANES), lambda bi, i: (bi, 0, 0))
    return pl.pallas_call(
        functools.partial(_dsa_kernel, q0=q0, k_lo=k_lo, k_hi=k_hi, topk=topk),
        grid=(b, nq // tq),
        in_specs=[qspec(iq.shape[2]), qspec(LANES), qspec(aq.shape[2]), kspec, kspec],
        out_specs=qspec(aq.shape[2]),
        out_shape=jax.ShapeDtypeStruct(aq.shape, F32),
        scratch_shapes=[pltpu.VMEM((kp, tq), I32),
                        pltpu.VMEM((IDX_HEADS // 2, 2 * tq, LANES), BF16),
                        pltpu.VMEM((N_HEADS // 2, 2 * tq, LANES), BF16),
                        pltpu.VMEM((N_HEADS // 2, LANES, 2 * tq), F32)],
        compiler_params=pltpu.CompilerParams(
            dimension_semantics=("parallel", "parallel"), vmem_limit_bytes=VMEM_LIMIT),
        name="dsa_attn",
    )(iq, wq, aq, ik, kv)


def _sb_kernel(q_ref, k_ref, v_ref, out_ref, *, q0, k_lo):
    tq, width = q_ref.shape
    npair = width // LANES
    nh = 2 * npair
    i = pl.program_id(1)
    qbase = q0 + i * tq
    pq = qbase + _row_iota((tq, 1))
    tk = SB_TK
    jmax = (qbase + tq - 1) // tk
    lane = _lane_iota((tq, LANES))
    low = lane < HEAD_DH
    qh = []
    for p in range(npair):
        qpair = q_ref[:, p * LANES:(p + 1) * LANES] * (HEAD_DH ** -0.5)
        qh += [jnp.where(low, qpair, 0.0).astype(BF16), jnp.where(low, 0.0, qpair).astype(BF16)]
    ugt = (_row_iota((tk, tk)) > _lane_iota((tk, tk))).astype(BF16)

    def body(state):
        t, _, runs, accs = state
        j = jmax - t
        k0 = pl.multiple_of(j * tk, tk)
        kb = [k_ref[pl.ds(k0, tk), p * LANES:(p + 1) * LANES].astype(BF16) for p in range(npair)]
        vb = [v_ref[pl.ds(k0, tk), p * LANES:(p + 1) * LANES].astype(BF16) for p in range(npair)]
        pk = k0 + _lane_iota((tq, tk))
        strict = (pk < pq) & (pk >= k_lo)
        zs = [_nt(qh[h], kb[h // 2]) for h in range(nh)]
        lss = [jnp.minimum(-z, 0.0) - jnp.log(1.0 + jnp.exp(-jnp.abs(z))) for z in zs]
        lms = [jnp.where(strict, ls, 0.0) for ls in lss]
        parts = [_split3(lm) for lm in lms]
        prods = [[_mm(x, ugt) for x in part] for part in parts]
        laters = [pr[0] + pr[1] + pr[2] for pr in prods]
        aw = [jnp.exp(jnp.where(strict, zs[h] + lss[h] + laters[h] + runs[h], NEG)).astype(BF16)
              for h in range(nh)]
        pvs = [_mm(aw[h], vb[h // 2]) for h in range(nh)]
        runs = tuple(runs[h] + laters[h][:, 0:1] + lms[h][:, 0:1] for h in range(nh))
        accs = tuple(accs[p] + jnp.where(low, pvs[2 * p], pvs[2 * p + 1]) for p in range(npair))
        top = runs[0]
        for h in range(1, nh):
            top = jnp.maximum(top, runs[h])
        return t + 1, jnp.max(top) >= SB_UNDERFLOW, runs, accs

    init = (jnp.int32(0), jnp.bool_(True), tuple(jnp.zeros((tq, 1), F32) for _ in range(nh)),
            tuple(jnp.zeros((tq, LANES), F32) for _ in range(npair)))
    _, _, _, accs = lax.while_loop(lambda st: (st[0] <= jmax) & st[1], body, init)
    for p in range(npair):
        out_ref[:, p * LANES:(p + 1) * LANES] = accs[p]


def _sb(q, k, v, *, q0, k_lo):
    b, nq, w = q.shape
    kp = k.shape[1]
    return pl.pallas_call(
        functools.partial(_sb_kernel, q0=q0, k_lo=k_lo),
        grid=(b, nq // TQ),
        in_specs=[pl.BlockSpec((None, TQ, w), lambda bi, i: (bi, i, 0)),
                  pl.BlockSpec((None, kp, w), lambda bi, i: (bi, 0, 0)),
                  pl.BlockSpec((None, kp, w), lambda bi, i: (bi, 0, 0))],
        out_specs=pl.BlockSpec((None, TQ, w), lambda bi, i: (bi, i, 0)),
        out_shape=jax.ShapeDtypeStruct(q.shape, F32),
        compiler_params=pltpu.CompilerParams(
            dimension_semantics=("parallel", "parallel"), vmem_limit_bytes=VMEM_LIMIT),
        name="sb_attn",
    )(q, k, v)


def _swa_kernel(sink_ref, q_ref, kv0_ref, kv1_ref, out_ref, qh_ref, *, q0, k_lo, k_hi):
    tq = q_ref.shape[0]
    i = pl.program_id(1)
    qbase = q0 + i * tq
    cq = (qbase + _row_iota((tq, 1))) // CHUNK
    pk = qbase + _lane_iota((tq, 2 * tq))
    ck = pk // CHUNK - tq // CHUNK
    mask = (ck <= cq) & (ck >= cq - WIN_CHUNKS) & (pk >= k_lo + tq) & (pk < k_hi + tq)
    _stage_heads(q_ref, qh_ref)
    kb0, va0 = _kv_tiles(kv0_ref[...])
    kb1, va1 = _kv_tiles(kv1_ref[...])
    kb = jnp.concatenate([kb0, kb1], axis=0)
    va = jnp.concatenate([va0, va1], axis=0)
    raw = [_nt(qh_ref[h], kb) for h in range(N_HEADS)]
    ms, ps = [], []
    for h in range(N_HEADS):
        s = jnp.where(mask, raw[h], NEG)
        m = jnp.maximum(jnp.max(s, axis=1, keepdims=True), sink_ref[h])
        ms.append(m)
        ps.append(jnp.exp(s - m).astype(BF16))
    pvs = [_mm(ps[h], va) for h in range(N_HEADS)]
    res = []
    for h in range(N_HEADS):
        den = pvs[h][:, HEAD_DH:HEAD_DH + 1] + jnp.exp(sink_ref[h] - ms[h])
        res.append(pvs[h] / jnp.broadcast_to(den, (tq, LANES)))
    _merge_heads(res, out_ref)


def _swa(sink, q, kv, *, q0, k_lo, k_hi):
    b, nq, w = q.shape
    return pl.pallas_call(
        functools.partial(_swa_kernel, q0=q0, k_lo=k_lo, k_hi=k_hi),
        grid=(b, nq // TQ),
        in_specs=[pl.BlockSpec(memory_space=pltpu.SMEM),
                  pl.BlockSpec((None, TQ, w), lambda bi, i: (bi, i, 0)),
                  pl.BlockSpec((None, TQ, LANES), lambda bi, i: (bi, i, 0)),
                  pl.BlockSpec((None, TQ, LANES), lambda bi, i: (bi, i + 1, 0))],
        out_specs=pl.BlockSpec((None, TQ, w), lambda bi, i: (bi, i, 0)),
        out_shape=jax.ShapeDtypeStruct(q.shape, F32),
        scratch_shapes=[pltpu.VMEM((N_HEADS, TQ, LANES), BF16)],
        compiler_params=pltpu.CompilerParams(
            dimension_semantics=("parallel", "parallel"), vmem_limit_bytes=VMEM_LIMIT),
        name="swa_attn",
    )(sink, q, kv, kv)


HG_LEVELS = (32, 16, 8, 4, 2, 1)
HG_BB = 8


def _hgrn_consts():
    t = np.arange(CHUNK)
    tri = (t[None, :] <= t[:, None]).astype(np.float32)
    rows = [tri]
    for m in HG_LEVELS:
        ref = (t // (2 * m)) * (2 * m) + m - 1
        rows.append(tri[ref])
    return np.concatenate(rows, axis=0)


def _hgrn_kernel(q_ref, f_ref, i_ref, lb_ref, gn_ref, gm_ref, s0_ref, o_ref, sT_ref, st_ref, *, p0, lo, hi):
    c = pl.program_id(1)
    nb = q_ref.shape[0]

    @pl.when(c == 0)
    def _():
        st_ref[...] = s0_ref[...]

    row = p0 + c * CHUNK + _row_iota((CHUNK, 1))
    valid = (row >= lo) & (row < hi)
    tt = _row_iota((CHUNK, CHUNK))
    ss = _lane_iota((CHUNK, CHUNK))
    lmask = [(tt // (2 * m) == ss // (2 * m)) & (tt % (2 * m) >= m) & (ss % (2 * m) < m) for m in HG_LEVELS]
    gmat = gm_ref[...]

    def per_batch(b, carry):
        heads = range(D_HEADS)
        cs = [slice(h * D_DK, (h + 1) * D_DK) for h in heads]
        q, kk, iv, parts = [], [], [], []
        for h in heads:
            lb = lb_ref[:, cs[h]]
            x = f_ref[b, :, cs[h]]
            e = jnp.exp(-jnp.abs(x))
            r = 1.0 / (1.0 + e)
            sig = jnp.where(x >= 0, r, e * r)
            nsig = jnp.where(x >= 0, e * r, r)
            parts.append(_split3(jnp.where(valid, jnp.log(lb + (1.0 - lb) * sig), 0.0)))
            kk.append(jnp.where(valid, (1.0 - lb) * nsig, 0.0))
            q.append(jnp.where(valid, q_ref[b, :, cs[h]], 0.0))
            iv.append(jnp.where(valid, i_ref[b, :, cs[h]], 0.0))
        prods = [[_mm(gmat, x) for x in parts[h]] for h in heads]
        cums = [pr[0] + pr[1] + pr[2] for pr in prods]
        cum = [c[0:CHUNK] for c in cums]
        last = [c[CHUNK - 1:CHUNK] for c in cums]
        qk = []
        for h in heads:
            for li in range(len(HG_LEVELS)):
                ref = cums[h][(li + 1) * CHUNK:(li + 2) * CHUNK]
                qk.append(((q[h] * jnp.exp(jnp.minimum(cum[h] - ref, 0.0))).astype(BF16),
                           (kk[h] * jnp.exp(jnp.minimum(ref - cum[h], 0.0))).astype(BF16)))
        lv = [_nt(a, c) for a, c in qk]
        nl = len(HG_LEVELS)
        att = []
        for h in heads:
            t = jnp.zeros((CHUNK, CHUNK), F32)
            for li in range(nl):
                t = t + jnp.where(lmask[li], lv[h * nl + li], 0.0)
            att.append(t.astype(BF16))
        ivb = [x.astype(BF16) for x in iv]
        stT = [st_ref[b, h] for h in heads]
        o_inter = [_nt((q[h] * jnp.exp(cum[h])).astype(BF16), stT[h].astype(BF16)) for h in heads]
        o_intra = [_mm(att[h], ivb[h]) for h in heads]
        upd = [_mm(iv[h].T.astype(BF16), (kk[h] * jnp.exp(last[h] - cum[h])).astype(BF16)) for h in heads]
        for h in heads:
            st_ref[b, h] = stT[h] * jnp.exp(last[h]) + upd[h]
            o = o_inter[h] + o_intra[h] + jnp.sum(q[h] * kk[h], axis=1, keepdims=True) * iv[h]
            ms = jnp.mean(o * o, axis=1, keepdims=True)
            o_ref[b, :, cs[h]] = o * lax.rsqrt(ms + EPS) * gn_ref[:, cs[h]]
        return carry
    lax.fori_loop(0, nb, per_batch, 0)

    @pl.when(c == pl.num_programs(1) - 1)
    def _():
        sT_ref[...] = st_ref[...]


def _hgrn(dq, df, di, lb, gn, s0T, *, p0, lo, hi):
    b, t, w = dq.shape
    bb = min(HG_BB, b)
    gm = jnp.asarray(_hgrn_consts(), BF16)
    row = pl.BlockSpec((bb, CHUNK, w), lambda bi, c: (bi, c, 0))
    vec = pl.BlockSpec((1, w), lambda bi, c: (0, 0))
    sspec = pl.BlockSpec((bb,) + s0T.shape[1:], lambda bi, c: (bi, 0, 0, 0))
    return pl.pallas_call(
        functools.partial(_hgrn_kernel, p0=p0, lo=lo, hi=hi),
        grid=(b // bb, t // CHUNK),
        in_specs=[row, row, row, vec, vec, pl.BlockSpec(gm.shape, lambda bi, c: (0, 0)), sspec],
        out_specs=[row, sspec],
        out_shape=[jax.ShapeDtypeStruct(dq.shape, F32), jax.ShapeDtypeStruct(s0T.shape, F32)],
        scratch_shapes=[pltpu.VMEM((bb,) + s0T.shape[1:], F32)],
        compiler_params=pltpu.CompilerParams(
            dimension_semantics=("parallel", "arbitrary"), vmem_limit_bytes=VMEM_LIMIT),
        name="hgrn2",
    )(dq, df, di, lb, gn, gm, s0T)


def _pick_tile(t, cap):
    best = 8
    for d in range(8, cap + 1, 8):
        if t % d == 0:
            best = d
    return best


def _pad_rows(x, front, total):
    return jnp.pad(x, ((0, 0), (front, total - front - x.shape[1]), (0, 0)))


def kernel(x_prompt, x_sample, cache_a_k, cache_a_v, cache_a_ik, cache_b_k, cache_b_v, cache_c_k, cache_c_v, state_d, meta_tokens, norm_g, final_g, w_in_even, w_out_even, w_in_odd, w_out_odd, c_sinks, d_lb_raw, d_norm_g):
    bp, seq, d = x_prompt.shape
    bs, tdec, _ = x_sample.shape
    depth = norm_g.shape[0]
    window = cache_c_k.shape[2]
    past = cache_a_k.shape[2] - N_META
    n = N_META + seq
    start = N_META + past
    topk_p = min(TOPK_MAX, seq // 4)
    topk_s = min(TOPK_MAX, (past + tdec) // 4)
    assert window == WIN_CHUNKS * CHUNK and TQ == WIN_CHUNKS * CHUNK

    np_ = -(-(FRONT + n) // SEQ_ALIGN) * SEQ_ALIGN
    p_lo, p_hi = FRONT, FRONT + n
    hp = jnp.concatenate([jnp.broadcast_to(meta_tokens[None].astype(F32), (bp, N_META, d)), x_prompt], axis=1)
    hp = _pad_rows(hp, FRONT, np_)
    tab_p = _rope_tables(jnp.arange(np_) - FRONT)
    tm_p = _pick_tile(np_, 384)

    s_lo, s_hi = FRONT, FRONT + start + tdec
    ks_len = -(-s_hi // SEQ_ALIGN) * SEQ_ALIGN
    qs0 = ((FRONT + start) // TQ) * TQ
    qoff = FRONT + start - qs0
    assert qoff + tdec <= TQ and qs0 + TQ <= ks_len
    rs = bs * tdec
    hs = x_sample.reshape(1, rs, d)
    tab_s = _rope_tables(jnp.tile(start + jnp.arange(tdec), bs))
    tm_s = _pick_tile(rs, 512)
    unflat = lambda a: a.reshape(bs, tdec, a.shape[-1])
    qtile = lambda a: _pad_rows(unflat(a), qoff, TQ)
    qa0 = ((FRONT + start) // DSA_TQ) * DSA_TQ
    qaoff = FRONT + start - qa0
    assert qaoff + tdec <= DSA_TQ and qa0 + DSA_TQ <= ks_len
    qtile_a = lambda a: _pad_rows(unflat(a), qaoff, DSA_TQ)

    lbp = jax.nn.softmax(d_lb_raw.astype(F32), axis=0)
    lower = jnp.cumsum(lbp, axis=0) - lbp[0]
    zpad = jnp.zeros((d, 88), F32)

    ev_p, ev_s, od_p, od_s = [], [], [], []
    for l in range(depth):
        j = l // 2
        final = l == depth - 1
        g = norm_g[l].astype(F32)[None]
        fg = final_g.astype(F32)[None]
        if l % 2 == 0:
            w = w_in_even[j]
            w = jnp.concatenate([w[:, :936], zpad, w[:, 936:]], axis=1).astype(BF16)
            wo = w_out_even[j].astype(BF16)
            aq, kv, iq, ikw, ag, bq, bk, bk16, bv, bv16, bg = _proj(hp, g, w, tab_p, EVEN_SEGS, tm_p)
            ao = _dsa(iq, ikw, aq, ikw, kv, q0=0, k_lo=p_lo, k_hi=p_hi, topk=topk_p)
            bo = _sb(bq, bk16, bv16, q0=0, k_lo=p_lo)
            hp = _outproj(hp, ao, ag, bo, bg, wo, fg, p_lo, p_hi, final, tm_p)
            ev_p.append((kv[:, p_lo:p_hi, :HEAD_DH], kv[:, p_lo:p_hi, HEAD_DH:], ikw[:, p_lo:p_hi, :IDX_DIM],
                         bk[:, p_lo:p_hi], bv[:, p_lo:p_hi]))
            aq, kv, iq, ikw, ag, bq, bk, bk16, bv, bv16, bg = _proj(hs, g, w, tab_s, EVEN_SEGS, tm_s)
            kv_n, ikw_n, bk_n, bv_n = unflat(kv), unflat(ikw), unflat(bk), unflat(bv)
            kv_c = jnp.concatenate([cache_a_k[j].reshape(bs, start, HEAD_DH),
                                    cache_a_v[j].reshape(bs, start, HEAD_DH)], axis=-1)
            ik_c = jnp.pad(cache_a_ik[j], ((0, 0), (0, 0), (0, LANES - IDX_DIM)))
            full = lambda cch, new: _pad_rows(jnp.concatenate([cch, new], axis=1), FRONT, ks_len)
            ao = _dsa(qtile_a(iq), qtile_a(ikw), qtile_a(aq), full(ik_c, ikw_n), full(kv_c, kv_n),
                      q0=qa0, k_lo=s_lo, k_hi=s_hi, topk=topk_s)
            bo = _sb(qtile(bq), full(cache_b_k[j].reshape(bs, start, -1).astype(BF16), unflat(bk16)),
                     full(cache_b_v[j].reshape(bs, start, -1).astype(BF16), unflat(bv16)), q0=qs0, k_lo=s_lo)
            flat = lambda a, o: a[:, o:o + tdec].reshape(1, rs, a.shape[-1])
            hs = _outproj(hs, flat(ao, qaoff), ag, flat(bo, qoff), bg, wo, fg, 0, rs, final, tm_s)
            ev_s.append((kv_n[..., :HEAD_DH], kv_n[..., HEAD_DH:], ikw_n[..., :IDX_DIM], bk_n, bv_n))
        else:
            w = w_in_odd[j].astype(BF16)
            wo = w_out_odd[j].astype(BF16)
            lb = lower[l][None]
            gn = jnp.tile(d_norm_g[j].astype(F32), D_HEADS)[None]
            sink = c_sinks[j].astype(F32)
            cq, kv, cg, dq, df, di, dg = _proj(hp, g, w, tab_p, ODD_SEGS, tm_p)
            co = _swa(sink, cq, jnp.pad(kv, ((0, 0), (TQ, 0), (0, 0))), q0=0, k_lo=p_lo, k_hi=p_hi)
            s0 = jnp.zeros((bp, D_HEADS, D_DK, D_DK), F32)
            do, sT = _hgrn(dq, df, di, lb, gn, s0, p0=0, lo=p_lo, hi=p_hi)
            hp = _outproj(hp, co, cg, do, dg, wo, fg, p_lo, p_hi, final, tm_p)
            od_p.append((kv[:, p_hi - window:p_hi, :HEAD_DH], kv[:, p_hi - window:p_hi, HEAD_DH:],
                         jnp.swapaxes(sT, 2, 3)))
            cq, kv, cg, dq, df, di, dg = _proj(hs, g, w, tab_s, ODD_SEGS, tm_s)
            kv_n = unflat(kv)
            kv_c = jnp.concatenate([cache_c_k[j].reshape(bs, window, HEAD_DH),
                                    cache_c_v[j].reshape(bs, window, HEAD_DH)], axis=-1)
            kv_w = jnp.concatenate([kv_c, kv_n], axis=1)
            kv_t = _pad_rows(kv_w, qoff, 2 * TQ)
            co = _swa(sink, qtile(cq), kv_t, q0=qs0, k_lo=s_hi - tdec - window, k_hi=s_hi)
            ctile = lambda a: _pad_rows(unflat(a), 0, CHUNK)
            do, sT = _hgrn(ctile(dq), ctile(df), ctile(di), lb, gn, jnp.swapaxes(state_d[j].astype(F32), 2, 3),
                           p0=0, lo=0, hi=tdec)
            flat = lambda a, o: a[:, o:o + tdec].reshape(1, rs, a.shape[-1])
            hs = _outproj(hs, flat(co, qoff), cg, flat(do, 0), dg, wo, fg, 0, rs, final, tm_s)
            od_s.append((kv_w[:, -window:, :HEAD_DH], kv_w[:, -window:, HEAD_DH:], jnp.swapaxes(sT, 2, 3)))

    y_prompt = hp[:, p_lo + N_META:p_hi]
    y_sample = hs.reshape(bs, tdec, d)
    stk = lambda group, idx, shape: jnp.stack([gp[idx] for gp in group]).reshape((len(group),) + shape)
    a_shape = lambda bb, tt: (bb, tt, 1, HEAD_DH)
    b_shape = lambda bb, tt: (bb, tt, N_HEADS, HEAD_DH)
    return (y_prompt, y_sample,
            stk(ev_p, 0, a_shape(bp, n)), stk(ev_s, 0, a_shape(bs, tdec)),
            stk(ev_p, 1, a_shape(bp, n)), stk(ev_s, 1, a_shape(bs, tdec)),
            stk(ev_p, 2, (bp, n, IDX_DIM)), stk(ev_s, 2, (bs, tdec, IDX_DIM)),
            stk(ev_p, 3, b_shape(bp, n)), stk(ev_s, 3, b_shape(bs, tdec)),
            stk(ev_p, 4, b_shape(bp, n)), stk(ev_s, 4, b_shape(bs, tdec)),
            stk(od_p, 0, a_shape(bp, window)), stk(od_s, 0, a_shape(bs, window)),
            stk(od_p, 1, a_shape(bp, window)), stk(od_s, 1, a_shape(bs, window)),
            stk(od_p, 2, (bp, D_HEADS, D_DK, D_DK)), stk(od_s, 2, (bs, D_HEADS, D_DK, D_DK)))
```

```python
import functools

import numpy as np
import jax
import jax.numpy as jnp
from jax import lax
from jax.experimental import pallas as pl
from jax.experimental.pallas import tpu as pltpu

F32 = jnp.float32
BF16 = jnp.bfloat16
I32 = jnp.int32

CHUNK = 64
N_META = 16
FRONT = CHUNK - N_META
ROPE_THETA = 500000.0
EPS = 1e-6
HEAD_DH = 64
N_HEADS = 8
IDX_HEADS = 8
IDX_DIM = 32
TOPK_MAX = 256
WIN_CHUNKS = 2
D_HEADS = 4
D_DK = 128

LANES = 128
TQ = 128
DSA_TQ = 256
TK = 256
SB_TK = 256
SEQ_ALIGN = 256
VMEM_LIMIT = 56 * 1024 * 1024

NEG = -1e30
LOG2E = 1.4426950408889634
SB_UNDERFLOW = -151.0
INT_MIN = -2 ** 31
INT_MAX = 2 ** 31 - 1


def _nt(a, b):
    return lax.dot_general(a, b, (((1,), (1,)), ((), ())), preferred_element_type=F32)


def _mm(a, b):
    return jnp.dot(a, b, preferred_element_type=F32)


def _split2(x):
    hi = x.astype(BF16)
    return hi, (x - hi.astype(F32)).astype(BF16)


def _lane_iota(shape):
    return lax.broadcasted_iota(I32, shape, len(shape) - 1)


def _row_iota(shape):
    return lax.broadcasted_iota(I32, shape, len(shape) - 2)


def _rope128(y, c, s1, s2, shift):
    return y * c + pltpu.roll(y, LANES - shift, 1) * s1 + pltpu.roll(y, shift, 1) * s2


def _proj_kernel(x_ref, g_ref, w_ref, tab_ref, *out_refs, segs, iw_scale):
    x = x_ref[...]
    ms = jnp.mean(x * x, axis=-1, keepdims=True)
    xn = (x * lax.rsqrt(ms + EPS) * g_ref[...]).astype(BF16)
    tm = x.shape[0]
    lane = _lane_iota((tm, LANES))
    outs = iter(out_refs)
    for c0, width, kind in segs:
        o_ref = next(outs)
        y = _mm(xn, w_ref[:, c0:c0 + width])
        if kind == "none":
            o_ref[...] = y
            continue
        if kind == "dual":
            o_ref[...] = y
            next(outs)[...] = y.astype(BF16)
            continue
        if kind in ("ropeA_q", "ropeA_kv"):
            c, s1, s2 = (tab_ref[:, i * LANES:(i + 1) * LANES] for i in (0, 1, 2))
            shift = HEAD_DH // 8
        else:
            c, s1, s2 = (tab_ref[:, i * LANES:(i + 1) * LANES] for i in (3, 4, 5))
            shift = IDX_DIM // 8
        if kind == "ropeA_kv":
            keep = lane < HEAD_DH
            c, s1, s2 = jnp.where(keep, c, 1.0), jnp.where(keep, s1, 0.0), jnp.where(keep, s2, 0.0)
        if kind == "ropeI_kw":
            keep = lane < IDX_DIM
            c = jnp.where(keep, c, jnp.where(lane < IDX_DIM + IDX_HEADS, iw_scale, 1.0))
            s1, s2 = jnp.where(keep, s1, 0.0), jnp.where(keep, s2, 0.0)
        for g in range(width // LANES):
            r = _rope128(y[:, g * LANES:(g + 1) * LANES], c, s1, s2, shift)
            if kind == "ropeA_q":
                r = r * (HEAD_DH ** -0.5)
            o_ref[:, g * LANES:(g + 1) * LANES] = r


def _proj(x, g, w, tab, segs, tm):
    b, t, d = x.shape
    pc = w.shape[1]
    kern = functools.partial(_proj_kernel, segs=segs, iw_scale=float(IDX_DIM ** -0.5 * IDX_HEADS ** -0.5))
    outs = []
    for _, wd, kind in segs:
        outs += [(wd, F32), (wd, BF16)] if kind == "dual" else [(wd, F32)]
    return pl.pallas_call(
        kern,
        grid=(b, t // tm),
        in_specs=[
            pl.BlockSpec((None, tm, d), lambda bi, i: (bi, i, 0)),
            pl.BlockSpec((1, d), lambda bi, i: (0, 0)),
            pl.BlockSpec((d, pc), lambda bi, i: (0, 0)),
            pl.BlockSpec((tm, tab.shape[1]), lambda bi, i: (i, 0)),
        ],
        out_specs=[pl.BlockSpec((None, tm, wd), lambda bi, i: (bi, i, 0)) for wd, _ in outs],
        out_shape=[jax.ShapeDtypeStruct((b, t, wd), dt) for wd, dt in outs],
        compiler_params=pltpu.CompilerParams(
            dimension_semantics=("parallel", "parallel"), vmem_limit_bytes=VMEM_LIMIT),
        name="in_proj",
    )(x, g, w, tab)


EVEN_SEGS = ((0, 512, "ropeA_q"), (512, 128, "ropeA_kv"), (640, 256, "ropeI_q"), (896, 128, "ropeI_kw"),
             (1024, 512, "none"), (1536, 512, "none"), (2048, 512, "dual"), (2560, 512, "dual"),
             (3072, 512, "none"))
ODD_SEGS = ((0, 512, "ropeA_q"), (512, 128, "ropeA_kv"), (640, 512, "none"), (1152, 512, "none"),
            (1664, 512, "none"), (2176, 512, "none"), (2688, 512, "none"))


def _rope_tables(pos):
    posf = pos.astype(F32)[:, None]
    lane = np.arange(LANES)
    out = []
    for dh in (HEAD_DH, IDX_DIM):
        r = dh // 4
        half = r // 2
        inv = ROPE_THETA ** (-(jnp.arange(half, dtype=F32) * 2.0 / r))
        ang = posf * inv[None, :]
        cos, sin = jnp.cos(ang), jnp.sin(ang)
        jj = lane % dh
        fi = jj % half
        cos_l, sin_l = cos[:, fi], sin[:, fi]
        out.append(jnp.where((jj < r)[None], cos_l, 1.0))
        out.append(jnp.where((jj < half)[None], -sin_l, 0.0))
        out.append(jnp.where(((jj >= half) & (jj < r))[None], sin_l, 0.0))
    return jnp.concatenate(out, axis=1)


def _silu(g):
    return g / (1.0 + jnp.exp(-g))


def _outproj_kernel(h_ref, o1_ref, g1_ref, o2_ref, g2_ref, w_ref, fg_ref, out_ref, *, lo, hi, final):
    tm = h_ref.shape[0]
    half = o1_ref.shape[1]
    m1 = (o1_ref[...] * _silu(g1_ref[...])).astype(BF16)
    m2 = (o2_ref[...] * _silu(g2_ref[...])).astype(BF16)
    hn = h_ref[...] + _mm(m1, w_ref[0:half, :]) + _mm(m2, w_ref[half:2 * half, :])
    if final:
        ms = jnp.mean(hn * hn, axis=-1, keepdims=True)
        hn = hn * lax.rsqrt(ms + EPS) * fg_ref[...]
    row = pl.program_id(1) * tm + _row_iota((tm, 1))
    out_ref[...] = jnp.where((row >= lo) & (row < hi), hn, 0.0)


def _outproj(h, o1, g1, o2, g2, w, fg, lo, hi, final, tm):
    b, t, d = h.shape
    half = o1.shape[2]
    row = lambda wd: pl.BlockSpec((None, tm, wd), lambda bi, i: (bi, i, 0))
    return pl.pallas_call(
        functools.partial(_outproj_kernel, lo=lo, hi=hi, final=final),
        grid=(b, t // tm),
        in_specs=[row(d), row(half), row(half), row(half), row(half),
                  pl.BlockSpec((2 * half, d), lambda bi, i: (0, 0)),
                  pl.BlockSpec((1, d), lambda bi, i: (0, 0))],
        out_specs=row(d),
        out_shape=jax.ShapeDtypeStruct((b, t, d), F32),
        compiler_params=pltpu.CompilerParams(
            dimension_semantics=("parallel", "parallel"), vmem_limit_bytes=VMEM_LIMIT),
        name="out_proj",
    )(h, o1, g1, o2, g2, w, fg)


def _stage_heads(q_ref, qh_ref):
    tq = q_ref.shape[0]
    lane = _lane_iota((tq, LANES))
    for hp in range(N_HEADS // 2):
        pair = q_ref[:, hp * LANES:(hp + 1) * LANES] * LOG2E
        qh_ref[2 * hp] = jnp.where(lane < HEAD_DH, pair, 0.0).astype(BF16)
        qh_ref[2 * hp + 1] = jnp.where(lane < HEAD_DH, pltpu.roll(pair, HEAD_DH, 1), 0.0).astype(BF16)


def _kv_tiles(kvt):
    lane = _lane_iota(kvt.shape)
    vaug = jnp.where(lane < HEAD_DH, pltpu.roll(kvt, HEAD_DH, 1), 1.0)
    return kvt.astype(BF16), vaug.astype(BF16)


def _merge_heads(res, out_ref):
    lane = _lane_iota(res[0].shape)
    for hp in range(N_HEADS // 2):
        out_ref[:, hp * LANES:(hp + 1) * LANES] = jnp.where(
            lane < HEAD_DH, res[2 * hp], pltpu.roll(res[2 * hp + 1], HEAD_DH, 1))


def _fold8(x, op):
    while x.shape[0] > 8:
        half = x.shape[0] // 2
        a, b = x[:half], x[half:]
        x = a + b if op == "sum" else (jnp.maximum(a, b) if op == "max" else jnp.minimum(a, b))
    return x


def _dsa_kernel(iq_ref, wq_ref, aq_ref, ik_ref, kv_ref, out_ref,
                sc_ref, iq2_ref, q2_ref, acc_ref, *, q0, k_lo, k_hi, topk):
    tq = iq_ref.shape[0]
    kp = ik_ref.shape[0]
    npair = N_HEADS // 2
    i = pl.program_id(1)
    qbase = q0 + i * tq
    nk = jnp.minimum(kp // TK, (qbase + tq + TK - 1) // TK)
    cq = (qbase + _lane_iota((1, tq))) // CHUNK
    n_adm = jnp.clip((cq + 1) * CHUNK, k_lo, k_hi) - k_lo
    kq = jnp.minimum(topk, n_adm).astype(F32)

    lane = _lane_iota((tq, LANES))
    for g in range(IDX_HEADS * IDX_DIM // LANES):
        grp = iq_ref[:, g * LANES:(g + 1) * LANES]
        for o in range(LANES // IDX_DIM):
            h = g * (LANES // IDX_DIM) + o
            sh = grp if o == 0 else pltpu.roll(grp, LANES - o * IDX_DIM, 1)
            iq2_ref[h // 2, (h % 2) * tq:(h % 2 + 1) * tq, :] = jnp.where(lane < IDX_DIM, sh, 0.0).astype(BF16)
    for p in range(npair):
        pair = aq_ref[:, p * LANES:(p + 1) * LANES] * LOG2E
        q2_ref[p, 0:tq, :] = jnp.where(lane < HEAD_DH, pair, 0.0).astype(BF16)
        q2_ref[p, tq:2 * tq, :] = jnp.where(lane < HEAD_DH, pltpu.roll(pair, HEAD_DH, 1), 0.0).astype(BF16)
    wt = wq_ref[...].T
    wrow = [wt[IDX_DIM + h:IDX_DIM + h + 1, :] for h in range(IDX_HEADS)]

    def score_body(j, carry):
        k0 = pl.multiple_of(j * TK, TK)
        kib = ik_ref[pl.ds(k0, TK), :].astype(BF16)
        tot = jnp.zeros((TK, tq), F32)
        raw = [_nt(kib, iq2_ref[p]) for p in range(npair)]
        for p in range(npair):
            tot = tot + jnp.maximum(raw[p][:, 0:tq], 0.0) * wrow[2 * p]
            tot = tot + jnp.maximum(raw[p][:, tq:2 * tq], 0.0) * wrow[2 * p + 1]
        pk = k0 + _row_iota((TK, tq))
        adm = (pk // CHUNK <= cq) & (pk >= k_lo) & (pk < k_hi)
        bits = pltpu.bitcast(tot, I32)
        key = bits ^ ((bits >> 31) & 0x7FFFFFFF)
        sc_ref[pl.ds(k0, TK), :] = jnp.where(adm, key, INT_MIN)
        return carry
    lax.fori_loop(0, nk, score_body, 0)

    def count_ge(cand):
        def body(j, cnt):
            x = sc_ref[pl.ds(pl.multiple_of(j * TK, TK), TK), :]
            return cnt + _fold8(jnp.where(x >= cand, 1.0, 0.0), "sum")
        cnt = lax.fori_loop(0, nk, body, jnp.zeros((8, tq), F32))
        return jnp.sum(cnt, axis=0, keepdims=True)

    c0 = count_ge(jnp.zeros((1, tq), I32))
    base0 = jnp.where(c0 >= kq, 0, INT_MIN).astype(I32)
    cb0 = jnp.where(c0 >= kq, c0, 2.0 * kp)

    def bit_cond(st):
        bi, _, cb = st
        return (bi < 31) & (jnp.max(jnp.abs(cb - kq)) > 0.0)

    def bit_body(st):
        bi, base, cb = st
        cand = base | jnp.left_shift(jnp.int32(1), 30 - bi)
        c = count_ge(cand)
        ok = c >= kq
        return bi + 1, jnp.where(ok, cand, base), jnp.where(ok, c, cb)
    _, thr, cthr = lax.while_loop(bit_cond, bit_body, (jnp.int32(0), base0, cb0))
    has_tie = jnp.max(cthr - kq) > 0.0

    acc_ref[...] = jnp.zeros(acc_ref.shape, F32)
    rowv = _row_iota((LANES, TK))

    def attend(j, ms, sel):
        k0 = pl.multiple_of(j * TK, TK)
        kvt = kv_ref[pl.ds(k0, TK), :]
        kb = kvt.astype(BF16)
        vaug_t = jnp.where(rowv < HEAD_DH, pltpu.roll(kvt, HEAD_DH, 1).T, 1.0).astype(BF16)
        sel2 = jnp.concatenate([sel, sel], axis=1)
        raw = [_nt(kb, q2_ref[p]) for p in range(npair)]
        out, pexp = [], []
        for p in range(npair):
            s = jnp.where(sel2, raw[p], NEG)
            m_new = jnp.maximum(ms[p], jnp.max(_fold8(s, "max"), axis=0, keepdims=True))
            pexp.append(jnp.exp2(s - m_new).astype(BF16))
            out.append(m_new)
        pv = [_mm(vaug_t, pexp[p]) for p in range(npair)]
        for p in range(npair):
            acc_ref[p] = jnp.exp2(ms[p] - out[p]) * acc_ref[p] + pv[p]
        return tuple(out)

    ms0 = tuple(jnp.full((1, 2 * tq), NEG, F32) for _ in range(npair))

    def fast_loop():
        def body(j, ms):
            x = sc_ref[pl.ds(pl.multiple_of(j * TK, TK), TK), :]
            return attend(j, ms, x >= thr)
        return lax.fori_loop(0, nk, body, ms0)

    def tie_loop():
        need = kq - count_ge(thr + 1)
        lower = (_lane_iota((TK, TK)) < _row_iota((TK, TK))).astype(BF16)

        def body(j, st):
            ms, run = st
            x = sc_ref[pl.ds(pl.multiple_of(j * TK, TK), TK), :]
            eq = x == thr
            eqf = jnp.where(eq, 1.0, 0.0)
            before = _mm(lower, eqf.astype(BF16)) + run
            sel = (x > thr) | (eq & (before < need))
            run = run + jnp.sum(_fold8(eqf, "sum"), axis=0, keepdims=True)
            return attend(j, ms, sel), run
        ms, _ = lax.fori_loop(0, nk, body, (ms0, jnp.zeros((1, tq), F32)))
        return ms
    lax.cond(has_tie, tie_loop, fast_loop)

    for p in range(npair):
        a = acc_ref[p]
        o2 = jnp.concatenate([a[0:HEAD_DH, 0:tq] / a[HEAD_DH:HEAD_DH + 1, 0:tq],
                              a[0:HEAD_DH, tq:2 * tq] / a[HEAD_DH:HEAD_DH + 1, tq:2 * tq]], axis=0)
        out_ref[:, p * LANES:(p + 1) * LANES] = o2.T


def _dsa(iq, wq, aq, ik, kv, *, q0, k_lo, k_hi, topk):
    b, nq, _ = iq.shape
    kp = ik.shape[1]
    tq = DSA_TQ if nq % DSA_TQ == 0 else TQ
    qspec = lambda wd: pl.BlockSpec((None, tq, wd), lambda bi, i: (bi, i, 0))
    kspec = pl.BlockSpec((None, kp, LANES), lambda bi, i: (bi, 0, 0))
    return pl.pallas_call(
        functools.partial(_dsa_kernel, q0=q0, k_lo=k_lo, k_hi=k_hi, topk=topk),
        grid=(b, nq // tq),
        in_specs=[qspec(iq.shape[2]), qspec(LANES), qspec(aq.shape[2]), kspec, kspec],
        out_specs=qspec(aq.shape[2]),
        out_shape=jax.ShapeDtypeStruct(aq.shape, F32),
        scratch_shapes=[pltpu.VMEM((kp, tq), I32),
                        pltpu.VMEM((IDX_HEADS // 2, 2 * tq, LANES), BF16),
                        pltpu.VMEM((N_HEADS // 2, 2 * tq, LANES), BF16),
                        pltpu.VMEM((N_HEADS // 2, LANES, 2 * tq), F32)],
        compiler_params=pltpu.CompilerParams(
            dimension_semantics=("parallel", "parallel"), vmem_limit_bytes=VMEM_LIMIT),
        name="dsa_attn",
    )(iq, wq, aq, ik, kv)


def _sb_kernel(q_ref, k_ref, v_ref, out_ref, *, q0, k_lo):
    tq, width = q_ref.shape
    npair = width // LANES
    nh = 2 * npair
    i = pl.program_id(1)
    qbase = q0 + i * tq
    pq = qbase + _row_iota((tq, 1))
    tk = SB_TK
    jmax = (qbase + tq - 1) // tk
    lane = _lane_iota((tq, LANES))
    low = lane < HEAD_DH
    qh = []
    for p in range(npair):
        qpair = q_ref[:, p * LANES:(p + 1) * LANES] * (HEAD_DH ** -0.5 * LOG2E)
        qh += [jnp.where(low, qpair, 0.0).astype(BF16), jnp.where(low, 0.0, qpair).astype(BF16)]
    ugt = (_row_iota((tk, tk)) > _lane_iota((tk, tk))).astype(BF16)

    def body(state):
        t, _, runs, accs = state
        j = jmax - t
        k0 = pl.multiple_of(j * tk, tk)
        kb = [k_ref[pl.ds(k0, tk), p * LANES:(p + 1) * LANES].astype(BF16) for p in range(npair)]
        vb = [v_ref[pl.ds(k0, tk), p * LANES:(p + 1) * LANES].astype(BF16) for p in range(npair)]
        pk = k0 + _lane_iota((tq, tk))
        strict = (pk < pq) & (pk >= k_lo)
        zs = [_nt(qh[h], kb[h // 2]) for h in range(nh)]
        lss = [jnp.minimum(-z, 0.0) - jnp.log2(1.0 + jnp.exp2(-jnp.abs(z))) for z in zs]
        lms = [jnp.where(strict, ls, 0.0) for ls in lss]
        parts = [_split2(lm) for lm in lms]
        prods = [[_mm(x, ugt) for x in part] for part in parts]
        laters = [pr[0] + pr[1] for pr in prods]
        aw = [jnp.exp2(jnp.where(strict, zs[h] + lss[h] + laters[h] + runs[h], NEG)).astype(BF16)
              for h in range(nh)]
        pvs = [_mm(aw[h], vb[h // 2]) for h in range(nh)]
        runs = tuple(runs[h] + laters[h][:, 0:1] + lms[h][:, 0:1] for h in range(nh))
        accs = tuple(accs[p] + jnp.where(low, pvs[2 * p], pvs[2 * p + 1]) for p in range(npair))
        top = runs[0]
        for h in range(1, nh):
            top = jnp.maximum(top, runs[h])
        return t + 1, jnp.max(top) >= SB_UNDERFLOW, runs, accs

    init = (jnp.int32(0), jnp.bool_(True), tuple(jnp.zeros((tq, 1), F32) for _ in range(nh)),
            tuple(jnp.zeros((tq, LANES), F32) for _ in range(npair)))
    _, _, _, accs = lax.while_loop(lambda st: (st[0] <= jmax) & st[1], body, init)
    for p in range(npair):
        out_ref[:, p * LANES:(p + 1) * LANES] = accs[p]


def _sb(q, k, v, *, q0, k_lo):
    b, nq, w = q.shape
    kp = k.shape[1]
    return pl.pallas_call(
        functools.partial(_sb_kernel, q0=q0, k_lo=k_lo),
        grid=(b, nq // TQ),
        in_specs=[pl.BlockSpec((None, TQ, w), lambda bi, i: (bi, i, 0)),
                  pl.BlockSpec((None, kp, w), lambda bi, i: (bi, 0, 0)),
                  pl.BlockSpec((None, kp, w), lambda bi, i: (bi, 0, 0))],
        out_specs=pl.BlockSpec((None, TQ, w), lambda bi, i: (bi, i, 0)),
        out_shape=jax.ShapeDtypeStruct(q.shape, F32),
        compiler_params=pltpu.CompilerParams(
            dimension_semantics=("parallel", "parallel"), vmem_limit_bytes=VMEM_LIMIT),
        name="sb_attn",
    )(q, k, v)


def _swa_kernel(sink_ref, q_ref, kv0_ref, kv1_ref, out_ref, qh_ref, *, q0, k_lo, k_hi):
    tq = q_ref.shape[0]
    i = pl.program_id(1)
    qbase = q0 + i * tq
    cq = (qbase + _row_iota((tq, 1))) // CHUNK
    pk = qbase + _lane_iota((tq, 2 * tq))
    ck = pk // CHUNK - tq // CHUNK
    mask = (ck <= cq) & (ck >= cq - WIN_CHUNKS) & (pk >= k_lo + tq) & (pk < k_hi + tq)
    _stage_heads(q_ref, qh_ref)
    kb0, va0 = _kv_tiles(kv0_ref[...])
    kb1, va1 = _kv_tiles(kv1_ref[...])
    kb = jnp.concatenate([kb0, kb1], axis=0)
    va = jnp.concatenate([va0, va1], axis=0)
    raw = [_nt(qh_ref[h], kb) for h in range(N_HEADS)]
    ms, ps = [], []
    sinks = [sink_ref[h] * LOG2E for h in range(N_HEADS)]
    for h in range(N_HEADS):
        s = jnp.where(mask, raw[h], NEG)
        m = jnp.maximum(jnp.max(s, axis=1, keepdims=True), sinks[h])
        ms.append(m)
        ps.append(jnp.exp2(s - m).astype(BF16))
    pvs = [_mm(ps[h], va) for h in range(N_HEADS)]
    res = []
    for h in range(N_HEADS):
        den = pvs[h][:, HEAD_DH:HEAD_DH + 1] + jnp.exp2(sinks[h] - ms[h])
        res.append(pvs[h] / jnp.broadcast_to(den, (tq, LANES)))
    _merge_heads(res, out_ref)


def _swa(sink, q, kv, *, q0, k_lo, k_hi):
    b, nq, w = q.shape
    return pl.pallas_call(
        functools.partial(_swa_kernel, q0=q0, k_lo=k_lo, k_hi=k_hi),
        grid=(b, nq // TQ),
        in_specs=[pl.BlockSpec(memory_space=pltpu.SMEM),
                  pl.BlockSpec((None, TQ, w), lambda bi, i: (bi, i, 0)),
                  pl.BlockSpec((None, TQ, LANES), lambda bi, i: (bi, i, 0)),
                  pl.BlockSpec((None, TQ, LANES), lambda bi, i: (bi, i + 1, 0))],
        out_specs=pl.BlockSpec((None, TQ, w), lambda bi, i: (bi, i, 0)),
        out_shape=jax.ShapeDtypeStruct(q.shape, F32),
        scratch_shapes=[pltpu.VMEM((N_HEADS, TQ, LANES), BF16)],
        compiler_params=pltpu.CompilerParams(
            dimension_semantics=("parallel", "parallel"), vmem_limit_bytes=VMEM_LIMIT),
        name="swa_attn",
    )(sink, q, kv, kv)


HG_LEVELS = (32, 16, 8, 4, 2, 1)
HG_BB = 8


def _hgrn_consts():
    t = np.arange(CHUNK)
    tri = (t[None, :] <= t[:, None]).astype(np.float32)
    rows = [tri]
    for m in HG_LEVELS:
        ref = (t // (2 * m)) * (2 * m) + m - 1
        rows.append(tri[ref])
    return np.concatenate(rows, axis=0)


def _hgrn_kernel(q_ref, f_ref, i_ref, lb_ref, gn_ref, gm_ref, s0_ref, o_ref, sT_ref, st_ref, *, p0, lo, hi):
    c = pl.program_id(1)
    nb = q_ref.shape[0]

    @pl.when(c == 0)
    def _():
        st_ref[...] = s0_ref[...]

    row = p0 + c * CHUNK + _row_iota((CHUNK, 1))
    valid = (row >= lo) & (row < hi)
    tt = _row_iota((CHUNK, CHUNK))
    ss = _lane_iota((CHUNK, CHUNK))
    lmask = [(tt // (2 * m) == ss // (2 * m)) & (tt % (2 * m) >= m) & (ss % (2 * m) < m) for m in HG_LEVELS]
    gmat = gm_ref[...]

    def per_batch(b, carry):
        heads = range(D_HEADS)
        cs = [slice(h * D_DK, (h + 1) * D_DK) for h in heads]
        q, kk, iv, parts = [], [], [], []
        for h in heads:
            lb = lb_ref[:, cs[h]]
            x = f_ref[b, :, cs[h]]
            e = jnp.exp(-jnp.abs(x))
            r = 1.0 / (1.0 + e)
            sig = jnp.where(x >= 0, r, e * r)
            nsig = jnp.where(x >= 0, e * r, r)
            parts.append(_split2(jnp.where(valid, jnp.log2(lb + (1.0 - lb) * sig), 0.0)))
            kk.append(jnp.where(valid, (1.0 - lb) * nsig, 0.0))
            q.append(jnp.where(valid, q_ref[b, :, cs[h]], 0.0))
            iv.append(jnp.where(valid, i_ref[b, :, cs[h]], 0.0))
        prods = [[_mm(gmat, x) for x in parts[h]] for h in heads]
        cums = [pr[0] + pr[1] for pr in prods]
        cum = [c[0:CHUNK] for c in cums]
        last = [c[CHUNK - 1:CHUNK] for c in cums]
        qk = []
        for h in heads:
            for li in range(len(HG_LEVELS)):
                ref = cums[h][(li + 1) * CHUNK:(li + 2) * CHUNK]
                qk.append(((q[h] * jnp.exp2(jnp.minimum(cum[h] - ref, 0.0))).astype(BF16),
                           (kk[h] * jnp.exp2(jnp.minimum(ref - cum[h], 0.0))).astype(BF16)))
        lv = [_nt(a, c) for a, c in qk]
        nl = len(HG_LEVELS)
        att = []
        for h in heads:
            t = jnp.zeros((CHUNK, CHUNK), F32)
            for li in range(nl):
                t = t + jnp.where(lmask[li], lv[h * nl + li], 0.0)
            att.append(t.astype(BF16))
        ivb = [x.astype(BF16) for x in iv]
        stT = [st_ref[b, h] for h in heads]
        o_inter = [_nt((q[h] * jnp.exp2(cum[h])).astype(BF16), stT[h].astype(BF16)) for h in heads]
        o_intra = [_mm(att[h], ivb[h]) for h in heads]
        upd = [_mm(iv[h].T.astype(BF16), (kk[h] * jnp.exp2(last[h] - cum[h])).astype(BF16)) for h in heads]
        for h in heads:
            st_ref[b, h] = stT[h] * jnp.exp2(last[h]) + upd[h]
            o = o_inter[h] + o_intra[h] + jnp.sum(q[h] * kk[h], axis=1, keepdims=True) * iv[h]
            ms = jnp.mean(o * o, axis=1, keepdims=True)
            o_ref[b, :, cs[h]] = o * lax.rsqrt(ms + EPS) * gn_ref[:, cs[h]]
        return carry
    lax.fori_loop(0, nb, per_batch, 0, unroll=2)

    @pl.when(c == pl.num_programs(1) - 1)
    def _():
        sT_ref[...] = st_ref[...]


def _hgrn(dq, df, di, lb, gn, s0T, *, p0, lo, hi):
    b, t, w = dq.shape
    bb = min(HG_BB, b)
    gm = jnp.asarray(_hgrn_consts(), BF16)
    row = pl.BlockSpec((bb, CHUNK, w), lambda bi, c: (bi, c, 0))
    vec = pl.BlockSpec((1, w), lambda bi, c: (0, 0))
    sspec = pl.BlockSpec((bb,) + s0T.shape[1:], lambda bi, c: (bi, 0, 0, 0))
    return pl.pallas_call(
        functools.partial(_hgrn_kernel, p0=p0, lo=lo, hi=hi),
        grid=(b // bb, t // CHUNK),
        in_specs=[row, row, row, vec, vec, pl.BlockSpec(gm.shape, lambda bi, c: (0, 0)), sspec],
        out_specs=[row, sspec],
        out_shape=[jax.ShapeDtypeStruct(dq.shape, F32), jax.ShapeDtypeStruct(s0T.shape, F32)],
        scratch_shapes=[pltpu.VMEM((bb,) + s0T.shape[1:], F32)],
        compiler_params=pltpu.CompilerParams(
            dimension_semantics=("parallel", "arbitrary"), vmem_limit_bytes=VMEM_LIMIT),
        name="hgrn2",
    )(dq, df, di, lb, gn, gm, s0T)


def _pick_tile(t, cap):
    best = 8
    for d in range(8, cap + 1, 8):
        if t % d == 0:
            best = d
    return best


def _pad_rows(x, front, total):
    return jnp.pad(x, ((0, 0), (front, total - front - x.shape[1]), (0, 0)))


def kernel(x_prompt, x_sample, cache_a_k, cache_a_v, cache_a_ik, cache_b_k, cache_b_v, cache_c_k, cache_c_v, state_d, meta_tokens, norm_g, final_g, w_in_even, w_out_even, w_in_odd, w_out_odd, c_sinks, d_lb_raw, d_norm_g):
    bp, seq, d = x_prompt.shape
    bs, tdec, _ = x_sample.shape
    depth = norm_g.shape[0]
    window = cache_c_k.shape[2]
    past = cache_a_k.shape[2] - N_META
    n = N_META + seq
    start = N_META + past
    topk_p = min(TOPK_MAX, seq // 4)
    topk_s = min(TOPK_MAX, (past + tdec) // 4)
    assert window == WIN_CHUNKS * CHUNK and TQ == WIN_CHUNKS * CHUNK

    np_ = -(-(FRONT + n) // SEQ_ALIGN) * SEQ_ALIGN
    p_lo, p_hi = FRONT, FRONT + n
    hp = jnp.concatenate([jnp.broadcast_to(meta_tokens[None].astype(F32), (bp, N_META, d)), x_prompt], axis=1)
    hp = _pad_rows(hp, FRONT, np_)
    tab_p = _rope_tables(jnp.arange(np_) - FRONT)
    tm_p = _pick_tile(np_, 384)

    s_lo, s_hi = FRONT, FRONT + start + tdec
    ks_len = -(-s_hi // SEQ_ALIGN) * SEQ_ALIGN
    qs0 = ((FRONT + start) // TQ) * TQ
    qoff = FRONT + start - qs0
    assert qoff + tdec <= TQ and qs0 + TQ <= ks_len
    rs = bs * tdec
    hs = x_sample.reshape(1, rs, d)
    tab_s = _rope_tables(jnp.tile(start + jnp.arange(tdec), bs))
    tm_s = _pick_tile(rs, 512)
    unflat = lambda a: a.reshape(bs, tdec, a.shape[-1])
    qtile = lambda a: _pad_rows(unflat(a), qoff, TQ)

    lbp = jax.nn.softmax(d_lb_raw.astype(F32), axis=0)
    lower = jnp.cumsum(lbp, axis=0) - lbp[0]
    zpad = jnp.zeros((d, 88), F32)

    ev_p, ev_s, od_p, od_s = [], [], [], []
    for l in range(depth):
        j = l // 2
        final = l == depth - 1
        g = norm_g[l].astype(F32)[None]
        fg = final_g.astype(F32)[None]
        if l % 2 == 0:
            w = w_in_even[j]
            w = jnp.concatenate([w[:, :936], zpad, w[:, 936:]], axis=1).astype(BF16)
            wo = w_out_even[j].astype(BF16)
            aq, kv, iq, ikw, ag, bq, bk, bk16, bv, bv16, bg = _proj(hp, g, w, tab_p, EVEN_SEGS, tm_p)
            ao = _dsa(iq, ikw, aq, ikw, kv, q0=0, k_lo=p_lo, k_hi=p_hi, topk=topk_p)
            bo = _sb(bq, bk16, bv16, q0=0, k_lo=p_lo)
            hp = _outproj(hp, ao, ag, bo, bg, wo, fg, p_lo, p_hi, final, tm_p)
            ev_p.append((kv[:, p_lo:p_hi, :HEAD_DH], kv[:, p_lo:p_hi, HEAD_DH:], ikw[:, p_lo:p_hi, :IDX_DIM],
                         bk[:, p_lo:p_hi], bv[:, p_lo:p_hi]))
            aq, kv, iq, ikw, ag, bq, bk, bk16, bv, bv16, bg = _proj(hs, g, w, tab_s, EVEN_SEGS, tm_s)
            kv_n, ikw_n, bk_n, bv_n = unflat(kv), unflat(ikw), unflat(bk), unflat(bv)
            kv_c = jnp.concatenate([cache_a_k[j].reshape(bs, start, HEAD_DH),
                                    cache_a_v[j].reshape(bs, start, HEAD_DH)], axis=-1)
            ik_c = jnp.pad(cache_a_ik[j], ((0, 0), (0, 0), (0, LANES - IDX_DIM)))
            full = lambda cch, new: _pad_rows(jnp.concatenate([cch, new], axis=1), FRONT, ks_len)
            ao = _dsa(qtile(iq), qtile(ikw), qtile(aq), full(ik_c, ikw_n), full(kv_c, kv_n),
                      q0=qs0, k_lo=s_lo, k_hi=s_hi, topk=topk_s)
            bo = _sb(qtile(bq), full(cache_b_k[j].reshape(bs, start, -1).astype(BF16), unflat(bk16)),
                     full(cache_b_v[j].reshape(bs, start, -1).astype(BF16), unflat(bv16)), q0=qs0, k_lo=s_lo)
            flat = lambda a, o: a[:, o:o + tdec].reshape(1, rs, a.shape[-1])
            hs = _outproj(hs, flat(ao, qoff), ag, flat(bo, qoff), bg, wo, fg, 0, rs, final, tm_s)
            ev_s.append((kv_n[..., :HEAD_DH], kv_n[..., HEAD_DH:], ikw_n[..., :IDX_DIM], bk_n, bv_n))
        else:
            w = w_in_odd[j].astype(BF16)
            wo = w_out_odd[j].astype(BF16)
            lb = lower[l][None]
            gn = jnp.tile(d_norm_g[j].astype(F32), D_HEADS)[None]
            sink = c_sinks[j].astype(F32)
            cq, kv, cg, dq, df, di, dg = _proj(hp, g, w, tab_p, ODD_SEGS, tm_p)
            co = _swa(sink, cq, jnp.pad(kv, ((0, 0), (TQ, 0), (0, 0))), q0=0, k_lo=p_lo, k_hi=p_hi)
            s0 = jnp.zeros((bp, D_HEADS, D_DK, D_DK), F32)
            do, sT = _hgrn(dq, df, di, lb, gn, s0, p0=0, lo=p_lo, hi=p_hi)
            hp = _outproj(hp, co, cg, do, dg, wo, fg, p_lo, p_hi, final, tm_p)
            od_p.append((kv[:, p_hi - window:p_hi, :HEAD_DH], kv[:, p_hi - window:p_hi, HEAD_DH:],
                         jnp.swapaxes(sT, 2, 3)))
            cq, kv, cg, dq, df, di, dg = _proj(hs, g, w, tab_s, ODD_SEGS, tm_s)
            kv_n = unflat(kv)
            kv_c = jnp.concatenate([cache_c_k[j].reshape(bs, window, HEAD_DH),
                                    cache_c_v[j].reshape(bs, window, HEAD_DH)], axis=-1)
            kv_w = jnp.concatenate([kv_c, kv_n], axis=1)
            kv_t = _pad_rows(kv_w, qoff, 2 * TQ)
            co = _swa(sink, qtile(cq), kv_t, q0=qs0, k_lo=s_hi - tdec - window, k_hi=s_hi)
            ctile = lambda a: _pad_rows(unflat(a), 0, CHUNK)
            do, sT = _hgrn(ctile(dq), ctile(df), ctile(di), lb, gn, jnp.swapaxes(state_d[j].astype(F32), 2, 3),
                           p0=0, lo=0, hi=tdec)
            flat = lambda a, o: a[:, o:o + tdec].reshape(1, rs, a.shape[-1])
            hs = _outproj(hs, flat(co, qoff), cg, flat(do, 0), dg, wo, fg, 0, rs, final, tm_s)
            od_s.append((kv_w[:, -window:, :HEAD_DH], kv_w[:, -window:, HEAD_DH:], jnp.swapaxes(sT, 2, 3)))

    y_prompt = hp[:, p_lo + N_META:p_hi]
    y_sample = hs.reshape(bs, tdec, d)
    stk = lambda group, idx, shape: jnp.stack([gp[idx] for gp in group]).reshape((len(group),) + shape)
    a_shape = lambda bb, tt: (bb, tt, 1, HEAD_DH)
    b_shape = lambda bb, tt: (bb, tt, N_HEADS, HEAD_DH)
    return (y_prompt, y_sample,
            stk(ev_p, 0, a_shape(bp, n)), stk(ev_s, 0, a_shape(bs, tdec)),
            stk(ev_p, 1, a_shape(bp, n)), stk(ev_s, 1, a_shape(bs, tdec)),
            stk(ev_p, 2, (bp, n, IDX_DIM)), stk(ev_s, 2, (bs, tdec, IDX_DIM)),
            stk(ev_p, 3, b_shape(bp, n)), stk(ev_s, 3, b_shape(bs, tdec)),
            stk(ev_p, 4, b_shape(bp, n)), stk(ev_s, 4, b_shape(bs, tdec)),
            stk(od_p, 0, a_shape(bp, window)), stk(od_s, 0, a_shape(bs, window)),
            stk(od_p, 1, a_shape(bp, window)), stk(od_s, 1, a_shape(bs, window)),
            stk(od_p, 2, (bp, D_HEADS, D_DK, D_DK)), stk(od_s, 2, (bs, D_HEADS, D_DK, D_DK)))
```

```python
import functools

import numpy as np
import jax
import jax.numpy as jnp
from jax import lax
from jax.experimental import pallas as pl
from jax.experimental.pallas import tpu as pltpu

F32 = jnp.float32
BF16 = jnp.bfloat16
I32 = jnp.int32

CHUNK = 64
N_META = 16
FRONT = CHUNK - N_META
ROPE_THETA = 500000.0
EPS = 1e-6
HEAD_DH = 64
N_HEADS = 8
IDX_HEADS = 8
IDX_DIM = 32
TOPK_MAX = 256
WIN_CHUNKS = 2
D_HEADS = 4
D_DK = 128

LANES = 128
TQ = 128
DSA_TQ = 256
TK = 256
SB_TK = 256
SEQ_ALIGN = 256
VMEM_LIMIT = 56 * 1024 * 1024

NEG = -1e30
LOG2E = 1.4426950408889634
SB_UNDERFLOW = -151.0
INT_MIN = -2 ** 31
INT_MAX = 2 ** 31 - 1
BITS_PER_CHECK = 3


def _nt(a, b):
    return lax.dot_general(a, b, (((1,), (1,)), ((), ())), preferred_element_type=F32)


def _mm(a, b):
    return jnp.dot(a, b, preferred_element_type=F32)


def _split2(x):
    hi = x.astype(BF16)
    return hi, (x - hi.astype(F32)).astype(BF16)


def _lane_iota(shape):
    return lax.broadcasted_iota(I32, shape, len(shape) - 1)


def _row_iota(shape):
    return lax.broadcasted_iota(I32, shape, len(shape) - 2)


def _rope128(y, c, s1, s2, shift):
    return y * c + pltpu.roll(y, LANES - shift, 1) * s1 + pltpu.roll(y, shift, 1) * s2


def _proj_kernel(x_ref, g_ref, w_ref, tab_ref, *out_refs, segs, iw_scale):
    x = x_ref[...]
    ms = jnp.mean(x * x, axis=-1, keepdims=True)
    xn = (x * lax.rsqrt(ms + EPS) * g_ref[...]).astype(BF16)
    tm = x.shape[0]
    lane = _lane_iota((tm, LANES))
    outs = iter(out_refs)
    for c0, width, kind in segs:
        o_ref = next(outs)
        y = _mm(xn, w_ref[:, c0:c0 + width])
        if kind == "none":
            o_ref[...] = y
            continue
        if kind == "dual":
            o_ref[...] = y
            next(outs)[...] = y.astype(BF16)
            continue
        if kind in ("ropeA_q", "ropeA_kv"):
            c, s1, s2 = (tab_ref[:, i * LANES:(i + 1) * LANES] for i in (0, 1, 2))
            shift = HEAD_DH // 8
        else:
            c, s1, s2 = (tab_ref[:, i * LANES:(i + 1) * LANES] for i in (3, 4, 5))
            shift = IDX_DIM // 8
        if kind == "ropeA_kv":
            keep = lane < HEAD_DH
            c, s1, s2 = jnp.where(keep, c, 1.0), jnp.where(keep, s1, 0.0), jnp.where(keep, s2, 0.0)
        if kind == "ropeI_kw":
            keep = lane < IDX_DIM
            c = jnp.where(keep, c, jnp.where(lane < IDX_DIM + IDX_HEADS, iw_scale, 1.0))
            s1, s2 = jnp.where(keep, s1, 0.0), jnp.where(keep, s2, 0.0)
        for g in range(width // LANES):
            r = _rope128(y[:, g * LANES:(g + 1) * LANES], c, s1, s2, shift)
            if kind == "ropeA_q":
                r = r * (HEAD_DH ** -0.5)
            o_ref[:, g * LANES:(g + 1) * LANES] = r


def _proj(x, g, w, tab, segs, tm):
    b, t, d = x.shape
    pc = w.shape[1]
    kern = functools.partial(_proj_kernel, segs=segs, iw_scale=float(IDX_DIM ** -0.5 * IDX_HEADS ** -0.5))
    outs = []
    for _, wd, kind in segs:
        outs += [(wd, F32), (wd, BF16)] if kind == "dual" else [(wd, F32)]
    return pl.pallas_call(
        kern,
        grid=(b, t // tm),
        in_specs=[
            pl.BlockSpec((None, tm, d), lambda bi, i: (bi, i, 0)),
            pl.BlockSpec((1, d), lambda bi, i: (0, 0)),
            pl.BlockSpec((d, pc), lambda bi, i: (0, 0)),
            pl.BlockSpec((tm, tab.shape[1]), lambda bi, i: (i, 0)),
        ],
        out_specs=[pl.BlockSpec((None, tm, wd), lambda bi, i: (bi, i, 0)) for wd, _ in outs],
        out_shape=[jax.ShapeDtypeStruct((b, t, wd), dt) for wd, dt in outs],
        compiler_params=pltpu.CompilerParams(
            dimension_semantics=("parallel", "parallel"), vmem_limit_bytes=VMEM_LIMIT),
        name="in_proj",
    )(x, g, w, tab)


EVEN_SEGS = ((0, 512, "ropeA_q"), (512, 128, "ropeA_kv"), (640, 256, "ropeI_q"), (896, 128, "ropeI_kw"),
             (1024, 512, "none"), (1536, 512, "none"), (2048, 512, "dual"), (2560, 512, "dual"),
             (3072, 512, "none"))
ODD_SEGS = ((0, 512, "ropeA_q"), (512, 128, "ropeA_kv"), (640, 512, "none"), (1152, 512, "none"),
            (1664, 512, "none"), (2176, 512, "none"), (2688, 512, "none"))


def _rope_tables(pos):
    posf = pos.astype(F32)[:, None]
    lane = np.arange(LANES)
    out = []
    for dh in (HEAD_DH, IDX_DIM):
        r = dh // 4
        half = r // 2
        inv = ROPE_THETA ** (-(jnp.arange(half, dtype=F32) * 2.0 / r))
        ang = posf * inv[None, :]
        cos, sin = jnp.cos(ang), jnp.sin(ang)
        jj = lane % dh
        fi = jj % half
        cos_l, sin_l = cos[:, fi], sin[:, fi]
        out.append(jnp.where((jj < r)[None], cos_l, 1.0))
        out.append(jnp.where((jj < half)[None], -sin_l, 0.0))
        out.append(jnp.where(((jj >= half) & (jj < r))[None], sin_l, 0.0))
    return jnp.concatenate(out, axis=1)


def _silu(g):
    return g / (1.0 + jnp.exp(-g))


def _outproj_kernel(h_ref, o1_ref, g1_ref, o2_ref, g2_ref, w_ref, fg_ref, out_ref, *, lo, hi, final):
    tm = h_ref.shape[0]
    half = o1_ref.shape[1]
    m1 = (o1_ref[...] * _silu(g1_ref[...])).astype(BF16)
    m2 = (o2_ref[...] * _silu(g2_ref[...])).astype(BF16)
    hn = h_ref[...] + _mm(m1, w_ref[0:half, :]) + _mm(m2, w_ref[half:2 * half, :])
    if final:
        ms = jnp.mean(hn * hn, axis=-1, keepdims=True)
        hn = hn * lax.rsqrt(ms + EPS) * fg_ref[...]
    row = pl.program_id(1) * tm + _row_iota((tm, 1))
    out_ref[...] = jnp.where((row >= lo) & (row < hi), hn, 0.0)


def _outproj(h, o1, g1, o2, g2, w, fg, lo, hi, final, tm):
    b, t, d = h.shape
    half = o1.shape[2]
    row = lambda wd: pl.BlockSpec((None, tm, wd), lambda bi, i: (bi, i, 0))
    return pl.pallas_call(
        functools.partial(_outproj_kernel, lo=lo, hi=hi, final=final),
        grid=(b, t // tm),
        in_specs=[row(d), row(half), row(half), row(half), row(half),
                  pl.BlockSpec((2 * half, d), lambda bi, i: (0, 0)),
                  pl.BlockSpec((1, d), lambda bi, i: (0, 0))],
        out_specs=row(d),
        out_shape=jax.ShapeDtypeStruct((b, t, d), F32),
        compiler_params=pltpu.CompilerParams(
            dimension_semantics=("parallel", "parallel"), vmem_limit_bytes=VMEM_LIMIT),
        name="out_proj",
    )(h, o1, g1, o2, g2, w, fg)


def _stage_heads(q_ref, qh_ref):
    tq = q_ref.shape[0]
    lane = _lane_iota((tq, LANES))
    for hp in range(N_HEADS // 2):
        pair = q_ref[:, hp * LANES:(hp + 1) * LANES] * LOG2E
        qh_ref[2 * hp] = jnp.where(lane < HEAD_DH, pair, 0.0).astype(BF16)
        qh_ref[2 * hp + 1] = jnp.where(lane < HEAD_DH, pltpu.roll(pair, HEAD_DH, 1), 0.0).astype(BF16)


def _kv_tiles(kvt):
    lane = _lane_iota(kvt.shape)
    vaug = jnp.where(lane < HEAD_DH, pltpu.roll(kvt, HEAD_DH, 1), 1.0)
    return kvt.astype(BF16), vaug.astype(BF16)


def _merge_heads(res, out_ref):
    lane = _lane_iota(res[0].shape)
    for hp in range(N_HEADS // 2):
        out_ref[:, hp * LANES:(hp + 1) * LANES] = jnp.where(
            lane < HEAD_DH, res[2 * hp], pltpu.roll(res[2 * hp + 1], HEAD_DH, 1))


def _fold8(x, op):
    while x.shape[0] > 8:
        half = x.shape[0] // 2
        a, b = x[:half], x[half:]
        x = a + b if op == "sum" else (jnp.maximum(a, b) if op == "max" else jnp.minimum(a, b))
    return x


def _dsa_kernel(iq_ref, wq_ref, aq_ref, ik_ref, kv_ref, out_ref,
                sc_ref, iq2_ref, q2_ref, acc_ref, *, q0, k_lo, k_hi, topk):
    tq = iq_ref.shape[0]
    kp = ik_ref.shape[0]
    npair = N_HEADS // 2
    i = pl.program_id(1)
    qbase = q0 + i * tq
    nk = jnp.minimum(kp // TK, (qbase + tq + TK - 1) // TK)
    cq = (qbase + _lane_iota((1, tq))) // CHUNK
    n_adm = jnp.clip((cq + 1) * CHUNK, k_lo, k_hi) - k_lo
    kq = jnp.minimum(topk, n_adm).astype(F32)

    lane = _lane_iota((tq, LANES))
    for g in range(IDX_HEADS * IDX_DIM // LANES):
        grp = iq_ref[:, g * LANES:(g + 1) * LANES]
        for o in range(LANES // IDX_DIM):
            h = g * (LANES // IDX_DIM) + o
            sh = grp if o == 0 else pltpu.roll(grp, LANES - o * IDX_DIM, 1)
            iq2_ref[h // 2, (h % 2) * tq:(h % 2 + 1) * tq, :] = jnp.where(lane < IDX_DIM, sh, 0.0).astype(BF16)
    for p in range(npair):
        pair = aq_ref[:, p * LANES:(p + 1) * LANES] * LOG2E
        q2_ref[p, 0:tq, :] = jnp.where(lane < HEAD_DH, pair, 0.0).astype(BF16)
        q2_ref[p, tq:2 * tq, :] = jnp.where(lane < HEAD_DH, pltpu.roll(pair, HEAD_DH, 1), 0.0).astype(BF16)
    wt = wq_ref[...].T
    wrow = [wt[IDX_DIM + h:IDX_DIM + h + 1, :] for h in range(IDX_HEADS)]

    def score_body(j, carry):
        k0 = pl.multiple_of(j * TK, TK)
        kib = ik_ref[pl.ds(k0, TK), :].astype(BF16)
        tot = jnp.zeros((TK, tq), F32)
        raw = [_nt(kib, iq2_ref[p]) for p in range(npair)]
        for p in range(npair):
            tot = tot + jnp.maximum(raw[p][:, 0:tq], 0.0) * wrow[2 * p]
            tot = tot + jnp.maximum(raw[p][:, tq:2 * tq], 0.0) * wrow[2 * p + 1]
        pk = k0 + _row_iota((TK, tq))
        adm = (pk // CHUNK <= cq) & (pk >= k_lo) & (pk < k_hi)
        bits = pltpu.bitcast(tot, I32)
        key = bits ^ ((bits >> 31) & 0x7FFFFFFF)
        sc_ref[pl.ds(k0, TK), :] = jnp.where(adm, key, INT_MIN)
        return carry
    lax.fori_loop(0, nk, score_body, 0)

    def count_ge(cand):
        def body(j, cnt):
            x = sc_ref[pl.ds(pl.multiple_of(j * TK, TK), TK), :]
            return cnt + _fold8(jnp.where(x >= cand, 1.0, 0.0), "sum")
        cnt = lax.fori_loop(0, nk, body, jnp.zeros((8, tq), F32))
        return jnp.sum(cnt, axis=0, keepdims=True)

    c0 = count_ge(jnp.zeros((1, tq), I32))
    base0 = jnp.where(c0 >= kq, 0, INT_MIN).astype(I32)
    cb0 = jnp.where(c0 >= kq, c0, 2.0 * kp)

    def bit_step(base, cb, shift):
        cand = base | jnp.left_shift(jnp.int32(1), shift)
        c = count_ge(cand)
        ok = c >= kq
        return jnp.where(ok, cand, base), jnp.where(ok, c, cb)

    def grp_cond(st):
        g, _, cb = st
        return (g < 30 // BITS_PER_CHECK) & (jnp.max(jnp.abs(cb - kq)) > 0.0)

    def grp_body(st):
        g, base, cb = st
        for r in range(BITS_PER_CHECK):
            base, cb = bit_step(base, cb, 29 - (BITS_PER_CHECK * g + r))
        return g + 1, base, cb
    base1, cb1 = bit_step(base0, cb0, jnp.int32(30))
    _, thr, cthr = lax.while_loop(grp_cond, grp_body, (jnp.int32(0), base1, cb1))
    has_tie = jnp.max(cthr - kq) > 0.0

    acc_ref[...] = jnp.zeros(acc_ref.shape, F32)
    rowv = _row_iota((LANES, TK))

    def attend(j, ms, sel):
        k0 = pl.multiple_of(j * TK, TK)
        kvt = kv_ref[pl.ds(k0, TK), :]
        kb = kvt.astype(BF16)
        vaug_t = jnp.where(rowv < HEAD_DH, pltpu.roll(kvt, HEAD_DH, 1).T, 1.0).astype(BF16)
        sel2 = jnp.concatenate([sel, sel], axis=1)
        raw = [_nt(kb, q2_ref[p]) for p in range(npair)]
        out, pexp = [], []
        for p in range(npair):
            s = jnp.where(sel2, raw[p], NEG)
            m_new = jnp.maximum(ms[p], jnp.max(_fold8(s, "max"), axis=0, keepdims=True))
            pexp.append(jnp.exp2(s - m_new).astype(BF16))
            out.append(m_new)
        pv = [_mm(vaug_t, pexp[p]) for p in range(npair)]
        for p in range(npair):
            acc_ref[p] = jnp.exp2(ms[p] - out[p]) * acc_ref[p] + pv[p]
        return tuple(out)

    ms0 = tuple(jnp.full((1, 2 * tq), NEG, F32) for _ in range(npair))

    def fast_loop():
        def body(j, ms):
            x = sc_ref[pl.ds(pl.multiple_of(j * TK, TK), TK), :]
            return attend(j, ms, x >= thr)
        return lax.fori_loop(0, nk, body, ms0)

    def tie_loop():
        need = kq - count_ge(thr + 1)
        lower = (_lane_iota((TK, TK)) < _row_iota((TK, TK))).astype(BF16)

        def body(j, st):
            ms, run = st
            x = sc_ref[pl.ds(pl.multiple_of(j * TK, TK), TK), :]
            eq = x == thr
            eqf = jnp.where(eq, 1.0, 0.0)
            before = _mm(lower, eqf.astype(BF16)) + run
            sel = (x > thr) | (eq & (before < need))
            run = run + jnp.sum(_fold8(eqf, "sum"), axis=0, keepdims=True)
            return attend(j, ms, sel), run
        ms, _ = lax.fori_loop(0, nk, body, (ms0, jnp.zeros((1, tq), F32)))
        return ms
    lax.cond(has_tie, tie_loop, fast_loop)

    for p in range(npair):
        a = acc_ref[p]
        o2 = jnp.concatenate([a[0:HEAD_DH, 0:tq] / a[HEAD_DH:HEAD_DH + 1, 0:tq],
                              a[0:HEAD_DH, tq:2 * tq] / a[HEAD_DH:HEAD_DH + 1, tq:2 * tq]], axis=0)
        out_ref[:, p * LANES:(p + 1) * LANES] = o2.T


def _dsa(iq, wq, aq, ik, kv, *, q0, k_lo, k_hi, topk):
    b, nq, _ = iq.shape
    kp = ik.shape[1]
    tq = DSA_TQ if nq % DSA_TQ == 0 else TQ
    qspec = lambda wd: pl.BlockSpec((None, tq, wd), lambda bi, i: (bi, i, 0))
    kspec = pl.BlockSpec((None, kp, LANES), lambda bi, i: (bi, 0, 0))
    return pl.pallas_call(
        functools.partial(_dsa_kernel, q0=q0, k_lo=k_lo, k_hi=k_hi, topk=topk),
        grid=(b, nq // tq),
        in_specs=[qspec(iq.shape[2]), qspec(LANES), qspec(aq.shape[2]), kspec, kspec],
        out_specs=qspec(aq.shape[2]),
        out_shape=jax.ShapeDtypeStruct(aq.shape, F32),
        scratch_shapes=[pltpu.VMEM((kp, tq), I32),
                        pltpu.VMEM((IDX_HEADS // 2, 2 * tq, LANES), BF16),
                        pltpu.VMEM((N_HEADS // 2, 2 * tq, LANES), BF16),
                        pltpu.VMEM((N_HEADS // 2, LANES, 2 * tq), F32)],
        compiler_params=pltpu.CompilerParams(
            dimension_semantics=("parallel", "parallel"), vmem_limit_bytes=VMEM_LIMIT),
        name="dsa_attn",
    )(iq, wq, aq, ik, kv)


def _sb_kernel(q_ref, k_ref, v_ref, out_ref, *, q0, k_lo):
    tq, width = q_ref.shape
    npair = width // LANES
    nh = 2 * npair
    i = pl.program_id(1)
    qbase = q0 + i * tq
    pq = qbase + _row_iota((tq, 1))
    tk = SB_TK
    jmax = (qbase + tq - 1) // tk
    lane = _lane_iota((tq, LANES))
    low = lane < HEAD_DH
    qh = []
    for p in range(npair):
        qpair = q_ref[:, p * LANES:(p + 1) * LANES] * (HEAD_DH ** -0.5 * LOG2E)
        qh += [jnp.where(low, qpair, 0.0).astype(BF16), jnp.where(low, 0.0, qpair).astype(BF16)]
    ugt = (_row_iota((tk, tk)) > _lane_iota((tk, tk))).astype(BF16)

    def body(state):
        t, _, runs, accs = state
        j = jmax - t
        k0 = pl.multiple_of(j * tk, tk)
        kb = [k_ref[pl.ds(k0, tk), p * LANES:(p + 1) * LANES].astype(BF16) for p in range(npair)]
        vb = [v_ref[pl.ds(k0, tk), p * LANES:(p + 1) * LANES].astype(BF16) for p in range(npair)]
        pk = k0 + _lane_iota((tq, tk))
        strict = (pk < pq) & (pk >= k_lo)
        zs = [_nt(qh[h], kb[h // 2]) for h in range(nh)]
        lss = [jnp.minimum(-z, 0.0) - jnp.log2(1.0 + jnp.exp2(-jnp.abs(z))) for z in zs]
        lms = [jnp.where(strict, ls, 0.0) for ls in lss]
        parts = [_split2(lm) for lm in lms]
        prods = [[_mm(x, ugt) for x in part] for part in parts]
        laters = [pr[0] + pr[1] for pr in prods]
        aw = [jnp.exp2(jnp.where(strict, zs[h] + lss[h] + laters[h] + runs[h], NEG)).astype(BF16)
              for h in range(nh)]
        pvs = [_mm(aw[h], vb[h // 2]) for h in range(nh)]
        runs = tuple(runs[h] + laters[h][:, 0:1] + lms[h][:, 0:1] for h in range(nh))
        accs = tuple(accs[p] + jnp.where(low, pvs[2 * p], pvs[2 * p + 1]) for p in range(npair))
        top = runs[0]
        for h in range(1, nh):
            top = jnp.maximum(top, runs[h])
        return t + 1, jnp.max(top) >= SB_UNDERFLOW, runs, accs

    init = (jnp.int32(0), jnp.bool_(True), tuple(jnp.zeros((tq, 1), F32) for _ in range(nh)),
            tuple(jnp.zeros((tq, LANES), F32) for _ in range(npair)))
    _, _, _, accs = lax.while_loop(lambda st: (st[0] <= jmax) & st[1], body, init)
    for p in range(npair):
        out_ref[:, p * LANES:(p + 1) * LANES] = accs[p]


def _sb(q, k, v, *, q0, k_lo):
    b, nq, w = q.shape
    kp = k.shape[1]
    return pl.pallas_call(
        functools.partial(_sb_kernel, q0=q0, k_lo=k_lo),
        grid=(b, nq // TQ),
        in_specs=[pl.BlockSpec((None, TQ, w), lambda bi, i: (bi, i, 0)),
                  pl.BlockSpec((None, kp, w), lambda bi, i: (bi, 0, 0)),
                  pl.BlockSpec((None, kp, w), lambda bi, i: (bi, 0, 0))],
        out_specs=pl.BlockSpec((None, TQ, w), lambda bi, i: (bi, i, 0)),
        out_shape=jax.ShapeDtypeStruct(q.shape, F32),
        compiler_params=pltpu.CompilerParams(
            dimension_semantics=("parallel", "parallel"), vmem_limit_bytes=VMEM_LIMIT),
        name="sb_attn",
    )(q, k, v)


def _swa_kernel(sink_ref, q_ref, kv0_ref, kv1_ref, out_ref, qh_ref, *, q0, k_lo, k_hi):
    tq = q_ref.shape[0]
    i = pl.program_id(1)
    qbase = q0 + i * tq
    cq = (qbase + _row_iota((tq, 1))) // CHUNK
    pk = qbase + _lane_iota((tq, 2 * tq))
    ck = pk // CHUNK - tq // CHUNK
    mask = (ck <= cq) & (ck >= cq - WIN_CHUNKS) & (pk >= k_lo + tq) & (pk < k_hi + tq)
    _stage_heads(q_ref, qh_ref)
    kb0, va0 = _kv_tiles(kv0_ref[...])
    kb1, va1 = _kv_tiles(kv1_ref[...])
    kb = jnp.concatenate([kb0, kb1], axis=0)
    va = jnp.concatenate([va0, va1], axis=0)
    raw = [_nt(qh_ref[h], kb) for h in range(N_HEADS)]
    ms, ps = [], []
    sinks = [sink_ref[h] * LOG2E for h in range(N_HEADS)]
    for h in range(N_HEADS):
        s = jnp.where(mask, raw[h], NEG)
        m = jnp.maximum(jnp.max(s, axis=1, keepdims=True), sinks[h])
        ms.append(m)
        ps.append(jnp.exp2(s - m).astype(BF16))
    pvs = [_mm(ps[h], va) for h in range(N_HEADS)]
    res = []
    for h in range(N_HEADS):
        den = pvs[h][:, HEAD_DH:HEAD_DH + 1] + jnp.exp2(sinks[h] - ms[h])
        res.append(pvs[h] / jnp.broadcast_to(den, (tq, LANES)))
    _merge_heads(res, out_ref)


def _swa(sink, q, kv, *, q0, k_lo, k_hi):
    b, nq, w = q.shape
    return pl.pallas_call(
        functools.partial(_swa_kernel, q0=q0, k_lo=k_lo, k_hi=k_hi),
        grid=(b, nq // TQ),
        in_specs=[pl.BlockSpec(memory_space=pltpu.SMEM),
                  pl.BlockSpec((None, TQ, w), lambda bi, i: (bi, i, 0)),
                  pl.BlockSpec((None, TQ, LANES), lambda bi, i: (bi, i, 0)),
                  pl.BlockSpec((None, TQ, LANES), lambda bi, i: (bi, i + 1, 0))],
        out_specs=pl.BlockSpec((None, TQ, w), lambda bi, i: (bi, i, 0)),
        out_shape=jax.ShapeDtypeStruct(q.shape, F32),
        scratch_shapes=[pltpu.VMEM((N_HEADS, TQ, LANES), BF16)],
        compiler_params=pltpu.CompilerParams(
            dimension_semantics=("parallel", "parallel"), vmem_limit_bytes=VMEM_LIMIT),
        name="swa_attn",
    )(sink, q, kv, kv)


HG_LEVELS = (32, 16, 8, 4, 2, 1)
HG_BB = 8


def _hgrn_consts():
    t = np.arange(CHUNK)
    tri = (t[None, :] <= t[:, None]).astype(np.float32)
    rows = [tri]
    for m in HG_LEVELS:
        ref = (t // (2 * m)) * (2 * m) + m - 1
        rows.append(tri[ref])
    return np.concatenate(rows, axis=0)


def _hgrn_kernel(q_ref, f_ref, i_ref, lb_ref, gn_ref, gm_ref, s0_ref, o_ref, sT_ref, st_ref, *, p0, lo, hi):
    c = pl.program_id(1)
    nb = q_ref.shape[0]

    @pl.when(c == 0)
    def _():
        st_ref[...] = s0_ref[...]

    row = p0 + c * CHUNK + _row_iota((CHUNK, 1))
    valid = (row >= lo) & (row < hi)
    tt = _row_iota((CHUNK, CHUNK))
    ss = _lane_iota((CHUNK, CHUNK))
    lmask = [(tt // (2 * m) == ss // (2 * m)) & (tt % (2 * m) >= m) & (ss % (2 * m) < m) for m in HG_LEVELS]
    gmat = gm_ref[...]

    def per_batch(b, carry):
        heads = range(D_HEADS)
        cs = [slice(h * D_DK, (h + 1) * D_DK) for h in heads]
        q, kk, iv, parts = [], [], [], []
        for h in heads:
            lb = lb_ref[:, cs[h]]
            x = f_ref[b, :, cs[h]]
            e = jnp.exp(-jnp.abs(x))
            r = 1.0 / (1.0 + e)
            sig = jnp.where(x >= 0, r, e * r)
            nsig = jnp.where(x >= 0, e * r, r)
            parts.append(_split2(jnp.where(valid, jnp.log2(lb + (1.0 - lb) * sig), 0.0)))
            kk.append(jnp.where(valid, (1.0 - lb) * nsig, 0.0))
            q.append(jnp.where(valid, q_ref[b, :, cs[h]], 0.0))
            iv.append(jnp.where(valid, i_ref[b, :, cs[h]], 0.0))
        prods = [[_mm(gmat, x) for x in parts[h]] for h in heads]
        cums = [pr[0] + pr[1] for pr in prods]
        cum = [c[0:CHUNK] for c in cums]
        last = [c[CHUNK - 1:CHUNK] for c in cums]
        qk = []
        for h in heads:
            for li in range(len(HG_LEVELS)):
                ref = cums[h][(li + 1) * CHUNK:(li + 2) * CHUNK]
                qk.append(((q[h] * jnp.exp2(jnp.minimum(cum[h] - ref, 0.0))).astype(BF16),
                           (kk[h] * jnp.exp2(jnp.minimum(ref - cum[h], 0.0))).astype(BF16)))
        lv = [_nt(a, c) for a, c in qk]
        nl = len(HG_LEVELS)
        att = []
        for h in heads:
            t = jnp.zeros((CHUNK, CHUNK), F32)
            for li in range(nl):
                t = t + jnp.where(lmask[li], lv[h * nl + li], 0.0)
            att.append(t.astype(BF16))
        ivb = [x.astype(BF16) for x in iv]
        stT = [st_ref[b, h] for h in heads]
        o_inter = [_nt((q[h] * jnp.exp2(cum[h])).astype(BF16), stT[h].astype(BF16)) for h in heads]
        o_intra = [_mm(att[h], ivb[h]) for h in heads]
        upd = [_mm(iv[h].T.astype(BF16), (kk[h] * jnp.exp2(last[h] - cum[h])).astype(BF16)) for h in heads]
        for h in heads:
            st_ref[b, h] = stT[h] * jnp.exp2(last[h]) + upd[h]
            o = o_inter[h] + o_intra[h] + jnp.sum(q[h] * kk[h], axis=1, keepdims=True) * iv[h]
            ms = jnp.mean(o * o, axis=1, keepdims=True)
            o_ref[b, :, cs[h]] = o * lax.rsqrt(ms + EPS) * gn_ref[:, cs[h]]
        return carry
    lax.fori_loop(0, nb, per_batch, 0, unroll=2)

    @pl.when(c == pl.num_programs(1) - 1)
    def _():
        sT_ref[...] = st_ref[...]


def _hgrn(dq, df, di, lb, gn, s0T, *, p0, lo, hi):
    b, t, w = dq.shape
    bb = min(HG_BB, b)
    gm = jnp.asarray(_hgrn_consts(), BF16)
    row = pl.BlockSpec((bb, CHUNK, w), lambda bi, c: (bi, c, 0))
    vec = pl.BlockSpec((1, w), lambda bi, c: (0, 0))
    sspec = pl.BlockSpec((bb,) + s0T.shape[1:], lambda bi, c: (bi, 0, 0, 0))
    return pl.pallas_call(
        functools.partial(_hgrn_kernel, p0=p0, lo=lo, hi=hi),
        grid=(b // bb, t // CHUNK),
        in_specs=[row, row, row, vec, vec, pl.BlockSpec(gm.shape, lambda bi, c: (0, 0)), sspec],
        out_specs=[row, sspec],
        out_shape=[jax.ShapeDtypeStruct(dq.shape, F32), jax.ShapeDtypeStruct(s0T.shape, F32)],
        scratch_shapes=[pltpu.VMEM((bb,) + s0T.shape[1:], F32)],
        compiler_params=pltpu.CompilerParams(
            dimension_semantics=("parallel", "arbitrary"), vmem_limit_bytes=VMEM_LIMIT),
        name="hgrn2",
    )(dq, df, di, lb, gn, gm, s0T)


def _pick_tile(t, cap):
    best = 8
    for d in range(8, cap + 1, 8):
        if t % d == 0:
            best = d
    return best


def _pad_rows(x, front, total):
    return jnp.pad(x, ((0, 0), (front, total - front - x.shape[1]), (0, 0)))


def kernel(x_prompt, x_sample, cache_a_k, cache_a_v, cache_a_ik, cache_b_k, cache_b_v, cache_c_k, cache_c_v, state_d, meta_tokens, norm_g, final_g, w_in_even, w_out_even, w_in_odd, w_out_odd, c_sinks, d_lb_raw, d_norm_g):
    bp, seq, d = x_prompt.shape
    bs, tdec, _ = x_sample.shape
    depth = norm_g.shape[0]
    window = cache_c_k.shape[2]
    past = cache_a_k.shape[2] - N_META
    n = N_META + seq
    start = N_META + past
    topk_p = min(TOPK_MAX, seq // 4)
    topk_s = min(TOPK_MAX, (past + tdec) // 4)
    assert window == WIN_CHUNKS * CHUNK and TQ == WIN_CHUNKS * CHUNK

    np_ = -(-(FRONT + n) // SEQ_ALIGN) * SEQ_ALIGN
    p_lo, p_hi = FRONT, FRONT + n
    hp = jnp.concatenate([jnp.broadcast_to(meta_tokens[None].astype(F32), (bp, N_META, d)), x_prompt], axis=1)
    hp = _pad_rows(hp, FRONT, np_)
    tab_p = _rope_tables(jnp.arange(np_) - FRONT)
    tm_p = _pick_tile(np_, 384)

    s_lo, s_hi = FRONT, FRONT + start + tdec
    ks_len = -(-s_hi // SEQ_ALIGN) * SEQ_ALIGN
    qs0 = ((FRONT + start) // TQ) * TQ
    qoff = FRONT + start - qs0
    assert qoff + tdec <= TQ and qs0 + TQ <= ks_len
    rs = bs * tdec
    hs = x_sample.reshape(1, rs, d)
    tab_s = _rope_tables(jnp.tile(start + jnp.arange(tdec), bs))
    tm_s = _pick_tile(rs, 512)
    unflat = lambda a: a.reshape(bs, tdec, a.shape[-1])
    qtile = lambda a: _pad_rows(unflat(a), qoff, TQ)

    lbp = jax.nn.softmax(d_lb_raw.astype(F32), axis=0)
    lower = jnp.cumsum(lbp, axis=0) - lbp[0]
    zpad = jnp.zeros((d, 88), F32)

    ev_p, ev_s, od_p, od_s = [], [], [], []
    for l in range(depth):
        j = l // 2
        final = l == depth - 1
        g = norm_g[l].astype(F32)[None]
        fg = final_g.astype(F32)[None]
        if l % 2 == 0:
            w = w_in_even[j]
            w = jnp.concatenate([w[:, :936], zpad, w[:, 936:]], axis=1).astype(BF16)
            wo = w_out_even[j].astype(BF16)
            aq, kv, iq, ikw, ag, bq, bk, bk16, bv, bv16, bg = _proj(hp, g, w, tab_p, EVEN_SEGS, tm_p)
            ao = _dsa(iq, ikw, aq, ikw, kv, q0=0, k_lo=p_lo, k_hi=p_hi, topk=topk_p)
            bo = _sb(bq, bk16, bv16, q0=0, k_lo=p_lo)
            hp = _outproj(hp, ao, ag, bo, bg, wo, fg, p_lo, p_hi, final, tm_p)
            ev_p.append((kv[:, p_lo:p_hi, :HEAD_DH], kv[:, p_lo:p_hi, HEAD_DH:], ikw[:, p_lo:p_hi, :IDX_DIM],
                         bk[:, p_lo:p_hi], bv[:, p_lo:p_hi]))
            aq, kv, iq, ikw, ag, bq, bk, bk16, bv, bv16, bg = _proj(hs, g, w, tab_s, EVEN_SEGS, tm_s)
            kv_n, ikw_n, bk_n, bv_n = unflat(kv), unflat(ikw), unflat(bk), unflat(bv)
            kv_c = jnp.concatenate([cache_a_k[j].reshape(bs, start, HEAD_DH),
                                    cache_a_v[j].reshape(bs, start, HEAD_DH)], axis=-1)
            ik_c = jnp.pad(cache_a_ik[j], ((0, 0), (0, 0), (0, LANES - IDX_DIM)))
            full = lambda cch, new: _pad_rows(jnp.concatenate([cch, new], axis=1), FRONT, ks_len)
            ao = _dsa(qtile(iq), qtile(ikw), qtile(aq), full(ik_c, ikw_n), full(kv_c, kv_n),
                      q0=qs0, k_lo=s_lo, k_hi=s_hi, topk=topk_s)
            bo = _sb(qtile(bq), full(cache_b_k[j].reshape(bs, start, -1).astype(BF16), unflat(bk16)),
                     full(cache_b_v[j].reshape(bs, start, -1).astype(BF16), unflat(bv16)), q0=qs0, k_lo=s_lo)
            flat = lambda a, o: a[:, o:o + tdec].reshape(1, rs, a.shape[-1])
            hs = _outproj(hs, flat(ao, qoff), ag, flat(bo, qoff), bg, wo, fg, 0, rs, final, tm_s)
            ev_s.append((kv_n[..., :HEAD_DH], kv_n[..., HEAD_DH:], ikw_n[..., :IDX_DIM], bk_n, bv_n))
        else:
            w = w_in_odd[j].astype(BF16)
            wo = w_out_odd[j].astype(BF16)
            lb = lower[l][None]
            gn = jnp.tile(d_norm_g[j].astype(F32), D_HEADS)[None]
            sink = c_sinks[j].astype(F32)
            cq, kv, cg, dq, df, di, dg = _proj(hp, g, w, tab_p, ODD_SEGS, tm_p)
            co = _swa(sink, cq, jnp.pad(kv, ((0, 0), (TQ, 0), (0, 0))), q0=0, k_lo=p_lo, k_hi=p_hi)
            s0 = jnp.zeros((bp, D_HEADS, D_DK, D_DK), F32)
            do, sT = _hgrn(dq, df, di, lb, gn, s0, p0=0, lo=p_lo, hi=p_hi)
            hp = _outproj(hp, co, cg, do, dg, wo, fg, p_lo, p_hi, final, tm_p)
            od_p.append((kv[:, p_hi - window:p_hi, :HEAD_DH], kv[:, p_hi - window:p_hi, HEAD_DH:],
                         jnp.swapaxes(sT, 2, 3)))
            cq, kv, cg, dq, df, di, dg = _proj(hs, g, w, tab_s, ODD_SEGS, tm_s)
            kv_n = unflat(kv)
            kv_c = jnp.concatenate([cache_c_k[j].reshape(bs, window, HEAD_DH),
                                    cache_c_v[j].reshape(bs, window, HEAD_DH)], axis=-1)
            kv_w = jnp.concatenate([kv_c, kv_n], axis=1)
            kv_t = _pad_rows(kv_w, qoff, 2 * TQ)
            co = _swa(sink, qtile(cq), kv_t, q0=qs0, k_lo=s_hi - tdec - window, k_hi=s_hi)
            ctile = lambda a: _pad_rows(unflat(a), 0, CHUNK)
            do, sT = _hgrn(ctile(dq), ctile(df), ctile(di), lb, gn, jnp.swapaxes(state_d[j].astype(F32), 2, 3),
                           p0=0, lo=0, hi=tdec)
            flat = lambda a, o: a[:, o:o + tdec].reshape(1, rs, a.shape[-1])
            hs = _outproj(hs, flat(co, qoff), cg, flat(do, 0), dg, wo, fg, 0, rs, final, tm_s)
            od_s.append((kv_w[:, -window:, :HEAD_DH], kv_w[:, -window:, HEAD_DH:], jnp.swapaxes(sT, 2, 3)))

    y_prompt = hp[:, p_lo + N_META:p_hi]
    y_sample = hs.reshape(bs, tdec, d)
    stk = lambda group, idx, shape: jnp.stack([gp[idx] for gp in group]).reshape((len(group),) + shape)
    a_shape = lambda bb, tt: (bb, tt, 1, HEAD_DH)
    b_shape = lambda bb, tt: (bb, tt, N_HEADS, HEAD_DH)
    return (y_prompt, y_sample,
            stk(ev_p, 0, a_shape(bp, n)), stk(ev_s, 0, a_shape(bs, tdec)),
            stk(ev_p, 1, a_shape(bp, n)), stk(ev_s, 1, a_shape(bs, tdec)),
            stk(ev_p, 2, (bp, n, IDX_DIM)), stk(ev_s, 2, (bs, tdec, IDX_DIM)),
            stk(ev_p, 3, b_shape(bp, n)), stk(ev_s, 3, b_shape(bs, tdec)),
            stk(ev_p, 4, b_shape(bp, n)), stk(ev_s, 4, b_shape(bs, tdec)),
            stk(od_p, 0, a_shape(bp, window)), stk(od_s, 0, a_shape(bs, window)),
            stk(od_p, 1, a_shape(bp, window)), stk(od_s, 1, a_shape(bs, window)),
            stk(od_p, 2, (bp, D_HEADS, D_DK, D_DK)), stk(od_s, 2, (bs, D_HEADS, D_DK, D_DK)))
```

```python
import functools

import numpy as np
import jax
import jax.numpy as jnp
from jax import lax
from jax.experimental import pallas as pl
from jax.experimental.pallas import tpu as pltpu

F32 = jnp.float32
BF16 = jnp.bfloat16
I32 = jnp.int32

CHUNK = 64
N_META = 16
FRONT = CHUNK - N_META
ROPE_THETA = 500000.0
EPS = 1e-6
HEAD_DH = 64
N_HEADS = 8
IDX_HEADS = 8
IDX_DIM = 32
TOPK_MAX = 256
WIN_CHUNKS = 2
D_HEADS = 4
D_DK = 128

LANES = 128
TQ = 128
DSA_TQ = 256
TK = 256
SB_TK = 256
SEQ_ALIGN = 256
VMEM_LIMIT = 56 * 1024 * 1024

NEG = -1e30
LOG2E = 1.4426950408889634
SB_UNDERFLOW = -151.0
INT_MIN = -2 ** 31
INT_MAX = 2 ** 31 - 1
BITS_PER_CHECK = 3


def _nt(a, b):
    return lax.dot_general(a, b, (((1,), (1,)), ((), ())), preferred_element_type=F32)


def _mm(a, b):
    return jnp.dot(a, b, preferred_element_type=F32)


def _split2(x):
    hi = x.astype(BF16)
    return hi, (x - hi.astype(F32)).astype(BF16)


def _lane_iota(shape):
    return lax.broadcasted_iota(I32, shape, len(shape) - 1)


def _row_iota(shape):
    return lax.broadcasted_iota(I32, shape, len(shape) - 2)


def _rope128(y, c, s1, s2, shift):
    return y * c + pltpu.roll(y, LANES - shift, 1) * s1 + pltpu.roll(y, shift, 1) * s2


def _proj_kernel(x_ref, g_ref, w_ref, tab_ref, *out_refs, segs, iw_scale):
    x = x_ref[...]
    ms = jnp.mean(x * x, axis=-1, keepdims=True)
    xn = (x * lax.rsqrt(ms + EPS) * g_ref[...]).astype(BF16)
    tm = x.shape[0]
    lane = _lane_iota((tm, LANES))
    outs = iter(out_refs)
    for c0, width, kind in segs:
        o_ref = next(outs)
        y = _mm(xn, w_ref[:, c0:c0 + width])
        if kind == "none":
            o_ref[...] = y
            continue
        if kind == "dual":
            o_ref[...] = y
            next(outs)[...] = y.astype(BF16)
            continue
        if kind in ("ropeA_q", "ropeA_kv"):
            c, s1, s2 = (tab_ref[:, i * LANES:(i + 1) * LANES] for i in (0, 1, 2))
            shift = HEAD_DH // 8
        else:
            c, s1, s2 = (tab_ref[:, i * LANES:(i + 1) * LANES] for i in (3, 4, 5))
            shift = IDX_DIM // 8
        if kind == "ropeA_kv":
            keep = lane < HEAD_DH
            c, s1, s2 = jnp.where(keep, c, 1.0), jnp.where(keep, s1, 0.0), jnp.where(keep, s2, 0.0)
        if kind == "ropeI_kw":
            keep = lane < IDX_DIM
            c = jnp.where(keep, c, jnp.where(lane < IDX_DIM + IDX_HEADS, iw_scale, 1.0))
            s1, s2 = jnp.where(keep, s1, 0.0), jnp.where(keep, s2, 0.0)
        for g in range(width // LANES):
            r = _rope128(y[:, g * LANES:(g + 1) * LANES], c, s1, s2, shift)
            if kind == "ropeA_q":
                r = r * (HEAD_DH ** -0.5)
            o_ref[:, g * LANES:(g + 1) * LANES] = r


def _proj(x, g, w, tab, segs, tm):
    b, t, d = x.shape
    pc = w.shape[1]
    kern = functools.partial(_proj_kernel, segs=segs, iw_scale=float(IDX_DIM ** -0.5 * IDX_HEADS ** -0.5))
    outs = []
    for _, wd, kind in segs:
        outs += [(wd, F32), (wd, BF16)] if kind == "dual" else [(wd, F32)]
    return pl.pallas_call(
        kern,
        grid=(b, t // tm),
        in_specs=[
            pl.BlockSpec((None, tm, d), lambda bi, i: (bi, i, 0)),
            pl.BlockSpec((1, d), lambda bi, i: (0, 0)),
            pl.BlockSpec((d, pc), lambda bi, i: (0, 0)),
            pl.BlockSpec((tm, tab.shape[1]), lambda bi, i: (i, 0)),
        ],
        out_specs=[pl.BlockSpec((None, tm, wd), lambda bi, i: (bi, i, 0)) for wd, _ in outs],
        out_shape=[jax.ShapeDtypeStruct((b, t, wd), dt) for wd, dt in outs],
        compiler_params=pltpu.CompilerParams(
            dimension_semantics=("parallel", "parallel"), vmem_limit_bytes=VMEM_LIMIT),
        name="in_proj",
    )(x, g, w, tab)


EVEN_SEGS = ((0, 512, "ropeA_q"), (512, 128, "ropeA_kv"), (640, 256, "ropeI_q"), (896, 128, "ropeI_kw"),
             (1024, 512, "none"), (1536, 512, "none"), (2048, 512, "dual"), (2560, 512, "dual"),
             (3072, 512, "none"))
ODD_SEGS = ((0, 512, "ropeA_q"), (512, 128, "ropeA_kv"), (640, 512, "none"), (1152, 512, "none"),
            (1664, 512, "none"), (2176, 512, "none"), (2688, 512, "none"))


def _rope_tables(pos):
    posf = pos.astype(F32)[:, None]
    lane = np.arange(LANES)
    out = []
    for dh in (HEAD_DH, IDX_DIM):
        r = dh // 4
        half = r // 2
        inv = ROPE_THETA ** (-(jnp.arange(half, dtype=F32) * 2.0 / r))
        ang = posf * inv[None, :]
        cos, sin = jnp.cos(ang), jnp.sin(ang)
        jj = lane % dh
        fi = jj % half
        cos_l, sin_l = cos[:, fi], sin[:, fi]
        out.append(jnp.where((jj < r)[None], cos_l, 1.0))
        out.append(jnp.where((jj < half)[None], -sin_l, 0.0))
        out.append(jnp.where(((jj >= half) & (jj < r))[None], sin_l, 0.0))
    return jnp.concatenate(out, axis=1)


def _silu(g):
    return g / (1.0 + jnp.exp(-g))


def _outproj_kernel(h_ref, o1_ref, g1_ref, o2_ref, g2_ref, w_ref, fg_ref, out_ref, *, lo, hi, final):
    tm = h_ref.shape[0]
    half = o1_ref.shape[1]
    m1 = (o1_ref[...] * _silu(g1_ref[...])).astype(BF16)
    m2 = (o2_ref[...] * _silu(g2_ref[...])).astype(BF16)
    hn = h_ref[...] + _mm(m1, w_ref[0:half, :]) + _mm(m2, w_ref[half:2 * half, :])
    if final:
        ms = jnp.mean(hn * hn, axis=-1, keepdims=True)
        hn = hn * lax.rsqrt(ms + EPS) * fg_ref[...]
    row = pl.program_id(1) * tm + _row_iota((tm, 1))
    out_ref[...] = jnp.where((row >= lo) & (row < hi), hn, 0.0)


def _outproj(h, o1, g1, o2, g2, w, fg, lo, hi, final, tm):
    b, t, d = h.shape
    half = o1.shape[2]
    row = lambda wd: pl.BlockSpec((None, tm, wd), lambda bi, i: (bi, i, 0))
    return pl.pallas_call(
        functools.partial(_outproj_kernel, lo=lo, hi=hi, final=final),
        grid=(b, t // tm),
        in_specs=[row(d), row(half), row(half), row(half), row(half),
                  pl.BlockSpec((2 * half, d), lambda bi, i: (0, 0)),
                  pl.BlockSpec((1, d), lambda bi, i: (0, 0))],
        out_specs=row(d),
        out_shape=jax.ShapeDtypeStruct((b, t, d), F32),
        compiler_params=pltpu.CompilerParams(
            dimension_semantics=("parallel", "parallel"), vmem_limit_bytes=VMEM_LIMIT),
        name="out_proj",
    )(h, o1, g1, o2, g2, w, fg)


def _stage_heads(q_ref, qh_ref):
    tq = q_ref.shape[0]
    lane = _lane_iota((tq, LANES))
    for hp in range(N_HEADS // 2):
        pair = q_ref[:, hp * LANES:(hp + 1) * LANES] * LOG2E
        qh_ref[2 * hp] = jnp.where(lane < HEAD_DH, pair, 0.0).astype(BF16)
        qh_ref[2 * hp + 1] = jnp.where(lane < HEAD_DH, pltpu.roll(pair, HEAD_DH, 1), 0.0).astype(BF16)


def _kv_tiles(kvt):
    lane = _lane_iota(kvt.shape)
    vaug = jnp.where(lane < HEAD_DH, pltpu.roll(kvt, HEAD_DH, 1), 1.0)
    return kvt.astype(BF16), vaug.astype(BF16)


def _merge_heads(res, out_ref):
    lane = _lane_iota(res[0].shape)
    for hp in range(N_HEADS // 2):
        out_ref[:, hp * LANES:(hp + 1) * LANES] = jnp.where(
            lane < HEAD_DH, res[2 * hp], pltpu.roll(res[2 * hp + 1], HEAD_DH, 1))


def _fold8(x, op):
    while x.shape[0] > 8:
        half = x.shape[0] // 2
        a, b = x[:half], x[half:]
        x = a + b if op == "sum" else (jnp.maximum(a, b) if op == "max" else jnp.minimum(a, b))
    return x


def _kv_prep_kernel(kv_ref, kb_ref, vt_ref):
    kvt = kv_ref[...]
    rowv = _row_iota((LANES, TK))
    kb_ref[...] = kvt.astype(BF16)
    vt_ref[...] = jnp.where(rowv < HEAD_DH, pltpu.roll(kvt, HEAD_DH, 1).T, 1.0).astype(BF16)


def _kv_prep(kv):
    b, kp, _ = kv.shape
    return pl.pallas_call(
        _kv_prep_kernel,
        grid=(b, kp // TK),
        in_specs=[pl.BlockSpec((None, TK, LANES), lambda bi, j: (bi, j, 0))],
        out_specs=[pl.BlockSpec((None, TK, LANES), lambda bi, j: (bi, j, 0)),
                   pl.BlockSpec((None, None, LANES, TK), lambda bi, j: (bi, j, 0, 0))],
        out_shape=[jax.ShapeDtypeStruct((b, kp, LANES), BF16),
                   jax.ShapeDtypeStruct((b, kp // TK, LANES, TK), BF16)],
        compiler_params=pltpu.CompilerParams(dimension_semantics=("parallel", "parallel")),
        name="kv_prep",
    )(kv)


def _dsa_kernel(iq_ref, wq_ref, aq_ref, ik_ref, kb_ref, vt_ref, out_ref,
                sc_ref, iq2_ref, q2_ref, acc_ref, *, q0, k_lo, k_hi, topk):
    tq = iq_ref.shape[0]
    kp = ik_ref.shape[0]
    npair = N_HEADS // 2
    i = pl.program_id(1)
    qbase = q0 + i * tq
    nk = jnp.minimum(kp // TK, (qbase + tq + TK - 1) // TK)
    cq = (qbase + _lane_iota((1, tq))) // CHUNK
    n_adm = jnp.clip((cq + 1) * CHUNK, k_lo, k_hi) - k_lo
    kq = jnp.minimum(topk, n_adm).astype(F32)

    lane = _lane_iota((tq, LANES))
    for g in range(IDX_HEADS * IDX_DIM // LANES):
        grp = iq_ref[:, g * LANES:(g + 1) * LANES]
        for o in range(LANES // IDX_DIM):
            h = g * (LANES // IDX_DIM) + o
            sh = grp if o == 0 else pltpu.roll(grp, LANES - o * IDX_DIM, 1)
            iq2_ref[h // 2, (h % 2) * tq:(h % 2 + 1) * tq, :] = jnp.where(lane < IDX_DIM, sh, 0.0).astype(BF16)
    for p in range(npair):
        pair = aq_ref[:, p * LANES:(p + 1) * LANES] * LOG2E
        q2_ref[p, 0:tq, :] = jnp.where(lane < HEAD_DH, pair, 0.0).astype(BF16)
        q2_ref[p, tq:2 * tq, :] = jnp.where(lane < HEAD_DH, pltpu.roll(pair, HEAD_DH, 1), 0.0).astype(BF16)
    wt = wq_ref[...].T
    wrow = [wt[IDX_DIM + h:IDX_DIM + h + 1, :] for h in range(IDX_HEADS)]

    def score_body(j, carry):
        k0 = pl.multiple_of(j * TK, TK)
        kib = ik_ref[pl.ds(k0, TK), :].astype(BF16)
        tot = jnp.zeros((TK, tq), F32)
        raw = [_nt(kib, iq2_ref[p]) for p in range(npair)]
        for p in range(npair):
            tot = tot + jnp.maximum(raw[p][:, 0:tq], 0.0) * wrow[2 * p]
            tot = tot + jnp.maximum(raw[p][:, tq:2 * tq], 0.0) * wrow[2 * p + 1]
        pk = k0 + _row_iota((TK, tq))
        adm = (pk // CHUNK <= cq) & (pk >= k_lo) & (pk < k_hi)
        bits = pltpu.bitcast(tot, I32)
        key = bits ^ ((bits >> 31) & 0x7FFFFFFF)
        sc_ref[pl.ds(k0, TK), :] = jnp.where(adm, key, INT_MIN)
        return carry
    lax.fori_loop(0, nk, score_body, 0)

    def count_ge(cand):
        def body(j, cnt):
            x = sc_ref[pl.ds(pl.multiple_of(j * TK, TK), TK), :]
            return cnt + _fold8(jnp.where(x >= cand, 1.0, 0.0), "sum")
        cnt = lax.fori_loop(0, nk, body, jnp.zeros((8, tq), F32))
        return jnp.sum(cnt, axis=0, keepdims=True)

    c0 = count_ge(jnp.zeros((1, tq), I32))
    base0 = jnp.where(c0 >= kq, 0, INT_MIN).astype(I32)
    cb0 = jnp.where(c0 >= kq, c0, 2.0 * kp)

    def bit_step(base, cb, shift):
        cand = base | jnp.left_shift(jnp.int32(1), shift)
        c = count_ge(cand)
        ok = c >= kq
        return jnp.where(ok, cand, base), jnp.where(ok, c, cb)

    def grp_cond(st):
        g, _, cb = st
        return (g < 30 // BITS_PER_CHECK) & (jnp.max(jnp.abs(cb - kq)) > 0.0)

    def grp_body(st):
        g, base, cb = st
        for r in range(BITS_PER_CHECK):
            base, cb = bit_step(base, cb, 29 - (BITS_PER_CHECK * g + r))
        return g + 1, base, cb
    base1, cb1 = bit_step(base0, cb0, jnp.int32(30))
    _, thr, cthr = lax.while_loop(grp_cond, grp_body, (jnp.int32(0), base1, cb1))
    has_tie = jnp.max(cthr - kq) > 0.0

    acc_ref[...] = jnp.zeros(acc_ref.shape, F32)

    def attend(j, ms, sel):
        k0 = pl.multiple_of(j * TK, TK)
        kb = kb_ref[pl.ds(k0, TK), :]
        vaug_t = vt_ref[j]
        sel2 = jnp.concatenate([sel, sel], axis=1)
        raw = [_nt(kb, q2_ref[p]) for p in range(npair)]
        out, pexp = [], []
        for p in range(npair):
            s = jnp.where(sel2, raw[p], NEG)
            m_new = jnp.maximum(ms[p], jnp.max(_fold8(s, "max"), axis=0, keepdims=True))
            pexp.append(jnp.exp2(s - m_new).astype(BF16))
            out.append(m_new)
        pv = [_mm(vaug_t, pexp[p]) for p in range(npair)]
        for p in range(npair):
            acc_ref[p] = jnp.exp2(ms[p] - out[p]) * acc_ref[p] + pv[p]
        return tuple(out)

    ms0 = tuple(jnp.full((1, 2 * tq), NEG, F32) for _ in range(npair))

    def fast_loop():
        def body(j, ms):
            x = sc_ref[pl.ds(pl.multiple_of(j * TK, TK), TK), :]
            return attend(j, ms, x >= thr)
        return lax.fori_loop(0, nk, body, ms0)

    def tie_loop():
        need = kq - count_ge(thr + 1)
        lower = (_lane_iota((TK, TK)) < _row_iota((TK, TK))).astype(BF16)

        def body(j, st):
            ms, run = st
            x = sc_ref[pl.ds(pl.multiple_of(j * TK, TK), TK), :]
            eq = x == thr
            eqf = jnp.where(eq, 1.0, 0.0)
            before = _mm(lower, eqf.astype(BF16)) + run
            sel = (x > thr) | (eq & (before < need))
            run = run + jnp.sum(_fold8(eqf, "sum"), axis=0, keepdims=True)
            return attend(j, ms, sel), run
        ms, _ = lax.fori_loop(0, nk, body, (ms0, jnp.zeros((1, tq), F32)))
        return ms
    lax.cond(has_tie, tie_loop, fast_loop)

    for p in range(npair):
        a = acc_ref[p]
        o2 = jnp.concatenate([a[0:HEAD_DH, 0:tq] / a[HEAD_DH:HEAD_DH + 1, 0:tq],
                              a[0:HEAD_DH, tq:2 * tq] / a[HEAD_DH:HEAD_DH + 1, tq:2 * tq]], axis=0)
        out_ref[:, p * LANES:(p + 1) * LANES] = o2.T


def _dsa(iq, wq, aq, ik, kv, *, q0, k_lo, k_hi, topk):
    b, nq, _ = iq.shape
    kp = ik.shape[1]
    tq = DSA_TQ if nq % DSA_TQ == 0 else TQ
    qspec = lambda wd: pl.BlockSpec((None, tq, wd), lambda bi, i: (bi, i, 0))
    kspec = pl.BlockSpec((None, kp, LANES), lambda bi, i: (bi, 0, 0))
    kb16, vt16 = _kv_prep(kv)
    return pl.pallas_call(
        functools.partial(_dsa_kernel, q0=q0, k_lo=k_lo, k_hi=k_hi, topk=topk),
        grid=(b, nq // tq),
        in_specs=[qspec(iq.shape[2]), qspec(LANES), qspec(aq.shape[2]), kspec, kspec,
                  pl.BlockSpec((None, kp // TK, LANES, TK), lambda bi, i: (bi, 0, 0, 0))],
        out_specs=qspec(aq.shape[2]),
        out_shape=jax.ShapeDtypeStruct(aq.shape, F32),
        scratch_shapes=[pltpu.VMEM((kp, tq), I32),
                        pltpu.VMEM((IDX_HEADS // 2, 2 * tq, LANES), BF16),
                        pltpu.VMEM((N_HEADS // 2, 2 * tq, LANES), BF16),
                        pltpu.VMEM((N_HEADS // 2, LANES, 2 * tq), F32)],
        compiler_params=pltpu.CompilerParams(
            dimension_semantics=("parallel", "parallel"), vmem_limit_bytes=VMEM_LIMIT),
        name="dsa_attn",
    )(iq, wq, aq, ik, kb16, vt16)


def _sb_kernel(q_ref, k_ref, v_ref, out_ref, *, q0, k_lo):
    tq, width = q_ref.shape
    npair = width // LANES
    nh = 2 * npair
    i = pl.program_id(1)
    qbase = q0 + i * tq
    pq = qbase + _row_iota((tq, 1))
    tk = SB_TK
    jmax = (qbase + tq - 1) // tk
    lane = _lane_iota((tq, LANES))
    low = lane < HEAD_DH
    qh = []
    for p in range(npair):
        qpair = q_ref[:, p * LANES:(p + 1) * LANES] * (HEAD_DH ** -0.5 * LOG2E)
        qh += [jnp.where(low, qpair, 0.0).astype(BF16), jnp.where(low, 0.0, qpair).astype(BF16)]
    ugt = (_row_iota((tk, tk)) > _lane_iota((tk, tk))).astype(BF16)

    def body(state):
        t, _, runs, accs = state
        j = jmax - t
        k0 = pl.multiple_of(j * tk, tk)
        kb = [k_ref[pl.ds(k0, tk), p * LANES:(p + 1) * LANES].astype(BF16) for p in range(npair)]
        vb = [v_ref[pl.ds(k0, tk), p * LANES:(p + 1) * LANES].astype(BF16) for p in range(npair)]
        pk = k0 + _lane_iota((tq, tk))
        strict = (pk < pq) & (pk >= k_lo)
        zs = [_nt(qh[h], kb[h // 2]) for h in range(nh)]
        lss = [jnp.minimum(-z, 0.0) - jnp.log2(1.0 + jnp.exp2(-jnp.abs(z))) for z in zs]
        lms = [jnp.where(strict, ls, 0.0) for ls in lss]
        parts = [_split2(lm) for lm in lms]
        prods = [[_mm(x, ugt) for x in part] for part in parts]
        laters = [pr[0] + pr[1] for pr in prods]
        aw = [jnp.exp2(jnp.where(strict, zs[h] + lss[h] + laters[h] + runs[h], NEG)).astype(BF16)
              for h in range(nh)]
        pvs = [_mm(aw[h], vb[h // 2]) for h in range(nh)]
        runs = tuple(runs[h] + laters[h][:, 0:1] + lms[h][:, 0:1] for h in range(nh))
        accs = tuple(accs[p] + jnp.where(low, pvs[2 * p], pvs[2 * p + 1]) for p in range(npair))
        top = runs[0]
        for h in range(1, nh):
            top = jnp.maximum(top, runs[h])
        return t + 1, jnp.max(top) >= SB_UNDERFLOW, runs, accs

    init = (jnp.int32(0), jnp.bool_(True), tuple(jnp.zeros((tq, 1), F32) for _ in range(nh)),
            tuple(jnp.zeros((tq, LANES), F32) for _ in range(npair)))
    _, _, _, accs = lax.while_loop(lambda st: (st[0] <= jmax) & st[1], body, init)
    for p in range(npair):
        out_ref[:, p * LANES:(p + 1) * LANES] = accs[p]


def _sb(q, k, v, *, q0, k_lo):
    b, nq, w = q.shape
    kp = k.shape[1]
    return pl.pallas_call(
        functools.partial(_sb_kernel, q0=q0, k_lo=k_lo),
        grid=(b, nq // TQ),
        in_specs=[pl.BlockSpec((None, TQ, w), lambda bi, i: (bi, i, 0)),
                  pl.BlockSpec((None, kp, w), lambda bi, i: (bi, 0, 0)),
                  pl.BlockSpec((None, kp, w), lambda bi, i: (bi, 0, 0))],
        out_specs=pl.BlockSpec((None, TQ, w), lambda bi, i: (bi, i, 0)),
        out_shape=jax.ShapeDtypeStruct(q.shape, F32),
        compiler_params=pltpu.CompilerParams(
            dimension_semantics=("parallel", "parallel"), vmem_limit_bytes=VMEM_LIMIT),
        name="sb_attn",
    )(q, k, v)


def _swa_kernel(sink_ref, q_ref, kv0_ref, kv1_ref, out_ref, qh_ref, *, q0, k_lo, k_hi):
    tq = q_ref.shape[0]
    i = pl.program_id(1)
    qbase = q0 + i * tq
    cq = (qbase + _row_iota((tq, 1))) // CHUNK
    pk = qbase + _lane_iota((tq, 2 * tq))
    ck = pk // CHUNK - tq // CHUNK
    mask = (ck <= cq) & (ck >= cq - WIN_CHUNKS) & (pk >= k_lo + tq) & (pk < k_hi + tq)
    _stage_heads(q_ref, qh_ref)
    kb0, va0 = _kv_tiles(kv0_ref[...])
    kb1, va1 = _kv_tiles(kv1_ref[...])
    kb = jnp.concatenate([kb0, kb1], axis=0)
    va = jnp.concatenate([va0, va1], axis=0)
    raw = [_nt(qh_ref[h], kb) for h in range(N_HEADS)]
    ms, ps = [], []
    sinks = [sink_ref[h] * LOG2E for h in range(N_HEADS)]
    for h in range(N_HEADS):
        s = jnp.where(mask, raw[h], NEG)
        m = jnp.maximum(jnp.max(s, axis=1, keepdims=True), sinks[h])
        ms.append(m)
        ps.append(jnp.exp2(s - m).astype(BF16))
    pvs = [_mm(ps[h], va) for h in range(N_HEADS)]
    res = []
    for h in range(N_HEADS):
        den = pvs[h][:, HEAD_DH:HEAD_DH + 1] + jnp.exp2(sinks[h] - ms[h])
        res.append(pvs[h] / jnp.broadcast_to(den, (tq, LANES)))
    _merge_heads(res, out_ref)


def _swa(sink, q, kv, *, q0, k_lo, k_hi):
    b, nq, w = q.shape
    return pl.pallas_call(
        functools.partial(_swa_kernel, q0=q0, k_lo=k_lo, k_hi=k_hi),
        grid=(b, nq // TQ),
        in_specs=[pl.BlockSpec(memory_space=pltpu.SMEM),
                  pl.BlockSpec((None, TQ, w), lambda bi, i: (bi, i, 0)),
                  pl.BlockSpec((None, TQ, LANES), lambda bi, i: (bi, i, 0)),
                  pl.BlockSpec((None, TQ, LANES), lambda bi, i: (bi, i + 1, 0))],
        out_specs=pl.BlockSpec((None, TQ, w), lambda bi, i: (bi, i, 0)),
        out_shape=jax.ShapeDtypeStruct(q.shape, F32),
        scratch_shapes=[pltpu.VMEM((N_HEADS, TQ, LANES), BF16)],
        compiler_params=pltpu.CompilerParams(
            dimension_semantics=("parallel", "parallel"), vmem_limit_bytes=VMEM_LIMIT),
        name="swa_attn",
    )(sink, q, kv, kv)


HG_LEVELS = (32, 16, 8, 4, 2, 1)
HG_BB = 8


def _hgrn_consts():
    t = np.arange(CHUNK)
    tri = (t[None, :] <= t[:, None]).astype(np.float32)
    rows = [tri]
    for m in HG_LEVELS:
        ref = (t // (2 * m)) * (2 * m) + m - 1
        rows.append(tri[ref])
    return np.concatenate(rows, axis=0)


def _hgrn_kernel(q_ref, f_ref, i_ref, lb_ref, gn_ref, gm_ref, s0_ref, o_ref, sT_ref, st_ref, *, p0, lo, hi):
    c = pl.program_id(1)
    nb = q_ref.shape[0]

    @pl.when(c == 0)
    def _():
        st_ref[...] = s0_ref[...]

    row = p0 + c * CHUNK + _row_iota((CHUNK, 1))
    valid = (row >= lo) & (row < hi)
    tt = _row_iota((CHUNK, CHUNK))
    ss = _lane_iota((CHUNK, CHUNK))
    lmask = [(tt // (2 * m) == ss // (2 * m)) & (tt % (2 * m) >= m) & (ss % (2 * m) < m) for m in HG_LEVELS]
    gmat = gm_ref[...]

    def per_batch(b, carry):
        heads = range(D_HEADS)
        cs = [slice(h * D_DK, (h + 1) * D_DK) for h in heads]
        q, kk, iv, parts = [], [], [], []
        for h in heads:
            lb = lb_ref[:, cs[h]]
            x = f_ref[b, :, cs[h]]
            e = jnp.exp(-jnp.abs(x))
            r = 1.0 / (1.0 + e)
            sig = jnp.where(x >= 0, r, e * r)
            nsig = jnp.where(x >= 0, e * r, r)
            parts.append(_split2(jnp.where(valid, jnp.log2(lb + (1.0 - lb) * sig), 0.0)))
            kk.append(jnp.where(valid, (1.0 - lb) * nsig, 0.0))
            q.append(jnp.where(valid, q_ref[b, :, cs[h]], 0.0))
            iv.append(jnp.where(valid, i_ref[b, :, cs[h]], 0.0))
        prods = [[_mm(gmat, x) for x in parts[h]] for h in heads]
        cums = [pr[0] + pr[1] for pr in prods]
        cum = [c[0:CHUNK] for c in cums]
        last = [c[CHUNK - 1:CHUNK] for c in cums]
        qk = []
        for h in heads:
            for li in range(len(HG_LEVELS)):
                ref = cums[h][(li + 1) * CHUNK:(li + 2) * CHUNK]
                qk.append(((q[h] * jnp.exp2(jnp.minimum(cum[h] - ref, 0.0))).astype(BF16),
                           (kk[h] * jnp.exp2(jnp.minimum(ref - cum[h], 0.0))).astype(BF16)))
        lv = [_nt(a, c) for a, c in qk]
        nl = len(HG_LEVELS)
        att = []
        for h in heads:
            t = jnp.zeros((CHUNK, CHUNK), F32)
            for li in range(nl):
                t = t + jnp.where(lmask[li], lv[h * nl + li], 0.0)
            att.append(t.astype(BF16))
        ivb = [x.astype(BF16) for x in iv]
        stT = [st_ref[b, h] for h in heads]
        o_inter = [_nt((q[h] * jnp.exp2(cum[h])).astype(BF16), stT[h].astype(BF16)) for h in heads]
        o_intra = [_mm(att[h], ivb[h]) for h in heads]
        upd = [_mm(iv[h].T.astype(BF16), (kk[h] * jnp.exp2(last[h] - cum[h])).astype(BF16)) for h in heads]
        for h in heads:
            st_ref[b, h] = stT[h] * jnp.exp2(last[h]) + upd[h]
            o = o_inter[h] + o_intra[h] + jnp.sum(q[h] * kk[h], axis=1, keepdims=True) * iv[h]
            ms = jnp.mean(o * o, axis=1, keepdims=True)
            o_ref[b, :, cs[h]] = o * lax.rsqrt(ms + EPS) * gn_ref[:, cs[h]]
        return carry
    lax.fori_loop(0, nb, per_batch, 0, unroll=2)

    @pl.when(c == pl.num_programs(1) - 1)
    def _():
        sT_ref[...] = st_ref[...]


def _hgrn(dq, df, di, lb, gn, s0T, *, p0, lo, hi):
    b, t, w = dq.shape
    bb = min(HG_BB, b)
    gm = jnp.asarray(_hgrn_consts(), BF16)
    row = pl.BlockSpec((bb, CHUNK, w), lambda bi, c: (bi, c, 0))
    vec = pl.BlockSpec((1, w), lambda bi, c: (0, 0))
    sspec = pl.BlockSpec((bb,) + s0T.shape[1:], lambda bi, c: (bi, 0, 0, 0))
    return pl.pallas_call(
        functools.partial(_hgrn_kernel, p0=p0, lo=lo, hi=hi),
        grid=(b // bb, t // CHUNK),
        in_specs=[row, row, row, vec, vec, pl.BlockSpec(gm.shape, lambda bi, c: (0, 0)), sspec],
        out_specs=[row, sspec],
        out_shape=[jax.ShapeDtypeStruct(dq.shape, F32), jax.ShapeDtypeStruct(s0T.shape, F32)],
        scratch_shapes=[pltpu.VMEM((bb,) + s0T.shape[1:], F32)],
        compiler_params=pltpu.CompilerParams(
            dimension_semantics=("parallel", "arbitrary"), vmem_limit_bytes=VMEM_LIMIT),
        name="hgrn2",
    )(dq, df, di, lb, gn, gm, s0T)


def _pick_tile(t, cap):
    best = 8
    for d in range(8, cap + 1, 8):
        if t % d == 0:
            best = d
    return best


def _pad_rows(x, front, total):
    return jnp.pad(x, ((0, 0), (front, total - front - x.shape[1]), (0, 0)))


def kernel(x_prompt, x_sample, cache_a_k, cache_a_v, cache_a_ik, cache_b_k, cache_b_v, cache_c_k, cache_c_v, state_d, meta_tokens, norm_g, final_g, w_in_even, w_out_even, w_in_odd, w_out_odd, c_sinks, d_lb_raw, d_norm_g):
    bp, seq, d = x_prompt.shape
    bs, tdec, _ = x_sample.shape
    depth = norm_g.shape[0]
    window = cache_c_k.shape[2]
    past = cache_a_k.shape[2] - N_META
    n = N_META + seq
    start = N_META + past
    topk_p = min(TOPK_MAX, seq // 4)
    topk_s = min(TOPK_MAX, (past + tdec) // 4)
    assert window == WIN_CHUNKS * CHUNK and TQ == WIN_CHUNKS * CHUNK

    np_ = -(-(FRONT + n) // SEQ_ALIGN) * SEQ_ALIGN
    p_lo, p_hi = FRONT, FRONT + n
    hp = jnp.concatenate([jnp.broadcast_to(meta_tokens[None].astype(F32), (bp, N_META, d)), x_prompt], axis=1)
    hp = _pad_rows(hp, FRONT, np_)
    tab_p = _rope_tables(jnp.arange(np_) - FRONT)
    tm_p = _pick_tile(np_, 384)

    s_lo, s_hi = FRONT, FRONT + start + tdec
    ks_len = -(-s_hi // SEQ_ALIGN) * SEQ_ALIGN
    qs0 = ((FRONT + start) // TQ) * TQ
    qoff = FRONT + start - qs0
    assert qoff + tdec <= TQ and qs0 + TQ <= ks_len
    rs = bs * tdec
    hs = x_sample.reshape(1, rs, d)
    tab_s = _rope_tables(jnp.tile(start + jnp.arange(tdec), bs))
    tm_s = _pick_tile(rs, 512)
    unflat = lambda a: a.reshape(bs, tdec, a.shape[-1])
    qtile = lambda a: _pad_rows(unflat(a), qoff, TQ)

    lbp = jax.nn.softmax(d_lb_raw.astype(F32), axis=0)
    lower = jnp.cumsum(lbp, axis=0) - lbp[0]
    zpad = jnp.zeros((d, 88), F32)

    ev_p, ev_s, od_p, od_s = [], [], [], []
    for l in range(depth):
        j = l // 2
        final = l == depth - 1
        g = norm_g[l].astype(F32)[None]
        fg = final_g.astype(F32)[None]
        if l % 2 == 0:
            w = w_in_even[j]
            w = jnp.concatenate([w[:, :936], zpad, w[:, 936:]], axis=1).astype(BF16)
            wo = w_out_even[j].astype(BF16)
            aq, kv, iq, ikw, ag, bq, bk, bk16, bv, bv16, bg = _proj(hp, g, w, tab_p, EVEN_SEGS, tm_p)
            ao = _dsa(iq, ikw, aq, ikw, kv, q0=0, k_lo=p_lo, k_hi=p_hi, topk=topk_p)
            bo = _sb(bq, bk16, bv16, q0=0, k_lo=p_lo)
            hp = _outproj(hp, ao, ag, bo, bg, wo, fg, p_lo, p_hi, final, tm_p)
            ev_p.append((kv[:, p_lo:p_hi, :HEAD_DH], kv[:, p_lo:p_hi, HEAD_DH:], ikw[:, p_lo:p_hi, :IDX_DIM],
                         bk[:, p_lo:p_hi], bv[:, p_lo:p_hi]))
            aq, kv, iq, ikw, ag, bq, bk, bk16, bv, bv16, bg = _proj(hs, g, w, tab_s, EVEN_SEGS, tm_s)
            kv_n, ikw_n, bk_n, bv_n = unflat(kv), unflat(ikw), unflat(bk), unflat(bv)
            kv_c = jnp.concatenate([cache_a_k[j].reshape(bs, start, HEAD_DH),
                                    cache_a_v[j].reshape(bs, start, HEAD_DH)], axis=-1)
            ik_c = jnp.pad(cache_a_ik[j], ((0, 0), (0, 0), (0, LANES - IDX_DIM)))
            full = lambda cch, new: _pad_rows(jnp.concatenate([cch, new], axis=1), FRONT, ks_len)
            ao = _dsa(qtile(iq), qtile(ikw), qtile(aq), full(ik_c, ikw_n), full(kv_c, kv_n),
                      q0=qs0, k_lo=s_lo, k_hi=s_hi, topk=topk_s)
            bo = _sb(qtile(bq), full(cache_b_k[j].reshape(bs, start, -1).astype(BF16), unflat(bk16)),
                     full(cache_b_v[j].reshape(bs, start, -1).astype(BF16), unflat(bv16)), q0=qs0, k_lo=s_lo)
            flat = lambda a, o: a[:, o:o + tdec].reshape(1, rs, a.shape[-1])
            hs = _outproj(hs, flat(ao, qoff), ag, flat(bo, qoff), bg, wo, fg, 0, rs, final, tm_s)
            ev_s.append((kv_n[..., :HEAD_DH], kv_n[..., HEAD_DH:], ikw_n[..., :IDX_DIM], bk_n, bv_n))
        else:
            w = w_in_odd[j].astype(BF16)
            wo = w_out_odd[j].astype(BF16)
            lb = lower[l][None]
            gn = jnp.tile(d_norm_g[j].astype(F32), D_HEADS)[None]
            sink = c_sinks[j].astype(F32)
            cq, kv, cg, dq, df, di, dg = _proj(hp, g, w, tab_p, ODD_SEGS, tm_p)
            co = _swa(sink, cq, jnp.pad(kv, ((0, 0), (TQ, 0), (0, 0))), q0=0, k_lo=p_lo, k_hi=p_hi)
            s0 = jnp.zeros((bp, D_HEADS, D_DK, D_DK), F32)
            do, sT = _hgrn(dq, df, di, lb, gn, s0, p0=0, lo=p_lo, hi=p_hi)
            hp = _outproj(hp, co, cg, do, dg, wo, fg, p_lo, p_hi, final, tm_p)
            od_p.append((kv[:, p_hi - window:p_hi, :HEAD_DH], kv[:, p_hi - window:p_hi, HEAD_DH:],
                         jnp.swapaxes(sT, 2, 3)))
            cq, kv, cg, dq, df, di, dg = _proj(hs, g, w, tab_s, ODD_SEGS, tm_s)
            kv_n = unflat(kv)
            kv_c = jnp.concatenate([cache_c_k[j].reshape(bs, window, HEAD_DH),
                                    cache_c_v[j].reshape(bs, window, HEAD_DH)], axis=-1)
            kv_w = jnp.concatenate([kv_c, kv_n], axis=1)
            kv_t = _pad_rows(kv_w, qoff, 2 * TQ)
            co = _swa(sink, qtile(cq), kv_t, q0=qs0, k_lo=s_hi - tdec - window, k_hi=s_hi)
            ctile = lambda a: _pad_rows(unflat(a), 0, CHUNK)
            do, sT = _hgrn(ctile(dq), ctile(df), ctile(di), lb, gn, jnp.swapaxes(state_d[j].astype(F32), 2, 3),
                           p0=0, lo=0, hi=tdec)
            flat = lambda a, o: a[:, o:o + tdec].reshape(1, rs, a.shape[-1])
            hs = _outproj(hs, flat(co, qoff), cg, flat(do, 0), dg, wo, fg, 0, rs, final, tm_s)
            od_s.append((kv_w[:, -window:, :HEAD_DH], kv_w[:, -window:, HEAD_DH:], jnp.swapaxes(sT, 2, 3)))

    y_prompt = hp[:, p_lo + N_META:p_hi]
    y_sample = hs.reshape(bs, tdec, d)
    stk = lambda group, idx, shape: jnp.stack([gp[idx] for gp in group]).reshape((len(group),) + shape)
    a_shape = lambda bb, tt: (bb, tt, 1, HEAD_DH)
    b_shape = lambda bb, tt: (bb, tt, N_HEADS, HEAD_DH)
    return (y_prompt, y_sample,
            stk(ev_p, 0, a_shape(bp, n)), stk(ev_s, 0, a_shape(bs, tdec)),
            stk(ev_p, 1, a_shape(bp, n)), stk(ev_s, 1, a_shape(bs, tdec)),
            stk(ev_p, 2, (bp, n, IDX_DIM)), stk(ev_s, 2, (bs, tdec, IDX_DIM)),
            stk(ev_p, 3, b_shape(bp, n)), stk(ev_s, 3, b_shape(bs, tdec)),
            stk(ev_p, 4, b_shape(bp, n)), stk(ev_s, 4, b_shape(bs, tdec)),
            stk(od_p, 0, a_shape(bp, window)), stk(od_s, 0, a_shape(bs, window)),
            stk(od_p, 1, a_shape(bp, window)), stk(od_s, 1, a_shape(bs, window)),
            stk(od_p, 2, (bp, D_HEADS, D_DK, D_DK)), stk(od_s, 2, (bs, D_HEADS, D_DK, D_DK)))
```
